```python
import jax, jax.numpy as jnp
from jax import lax
import numpy as np

D_MODEL = 2048
BATCH = 8
SEQ = 2048
DEPTH = 2
DEC_BATCH = 32
DEC_SEQ = 1
PAST_LEN = 8192
PAGE_SIZE = 128

N_A_LAYERS = DEPTH // 2
N_B_LAYERS = DEPTH - N_A_LAYERS
A_HEADS = 16
A_DK = 128
A_DV = D_MODEL // A_HEADS
A_CHUNK = 32
B_HEADS = 16
B_HEAD_DIM = 128
B_KV_HEADS = 4
B_WINDOWS = (128, 512, 2048)
B_DILATIONS = (1, 4, 16)
N_BRANCHES = 3
P_HEADS = 8
P_NKEYS = 128
P_EXPERTS = P_NKEYS * P_NKEYS
P_TOPK = 16
P_QDIM = 256
P_HALF = P_QDIM // 2
P_TOKEN_BLOCK = 128
EPS = 1e-6
F32 = jnp.float32

kernel_name = 'yoco_hgrn2_dilated_peer_step'


def rmsnorm(x, g):
    xf = x.astype(F32)
    y = xf * lax.rsqrt(jnp.mean(xf * xf, axis=-1, keepdims=True) + EPS) * g.astype(F32)
    return y.astype(x.dtype)


def alibi_slopes(n):
    return jnp.exp2(-8.0 * jnp.arange(1, n + 1, dtype=F32) / n)


def hgrn2_chunked(q, k, v, logf, s0):
    B, S, H, _ = q.shape
    nc = S // A_CHUNK

    def split(t):
        return t.reshape(B, nc, A_CHUNK, H, t.shape[-1]).swapaxes(0, 1)

    causal = jnp.tril(jnp.ones((A_CHUNK, A_CHUNK), dtype=bool))

    def step(s, xs):
        qc, kc, vc, lc = xs
        b = jnp.cumsum(lc, axis=1)
        qe = qc * jnp.exp(b)
        ke = kc * jnp.exp(-b)
        att = jnp.where(causal, jnp.einsum('bthk,bshk->bhts', qe, ke), 0.0)
        o = jnp.einsum('bthk,bhkv->bthv', qe, s) + jnp.einsum('bhts,bshv->bthv', att, vc)
        b_last = b[:, -1]
        s_new = jnp.exp(b_last)[..., None] * s + jnp.einsum('bshk,bshv->bhkv', kc * jnp.exp(b_last[:, None] - b), vc)
        return s_new, o

    s_fin, o = lax.scan(step, s0, (split(q), split(k), split(v), split(logf)))
    return o.swapaxes(0, 1).reshape(B, S, H, -1), s_fin


def hgrn2_recurrent(q, k, v, logf, s0):
    def step(s, xs):
        qt, kt, vt, lt = xs
        s = jnp.exp(lt)[..., None] * s + kt[..., None] * vt[:, :, None, :]
        return s, jnp.einsum('bhk,bhkv->bhv', qt, s)

    s_fin, o = lax.scan(step, s0, (q.swapaxes(0, 1), k.swapaxes(0, 1), v.swapaxes(0, 1), logf.swapaxes(0, 1)))
    return o.swapaxes(0, 1), s_fin


def hgrn2_mixer(xn, s0, w_in, lb, gnorm, w_out, prompt):
    B, T, _ = xn.shape
    q, f_raw, i, g = jnp.split(xn @ w_in, 4, axis=-1)
    f = lb + (1.0 - lb) * jax.nn.sigmoid(f_raw.astype(F32))

    def heads(t):
        return t.astype(F32).reshape(B, T, A_HEADS, -1)

    args = (heads(q), heads(1.0 - f), heads(i), heads(jnp.log(f)), s0.astype(F32))
    o, s_fin = hgrn2_chunked(*args) if prompt else hgrn2_recurrent(*args)
    o = rmsnorm(o, gnorm) * jax.nn.sigmoid(heads(g))
    return o.reshape(B, T, D_MODEL).astype(xn.dtype) @ w_out, s_fin


def dilated_window_prompt(q, k, v, r, steps, slopes):
    B, S, H, Dh = q.shape
    kvh = k.shape[2]
    g = H // kvh
    n = S // r
    nb = -(-n // steps)
    npad = nb * steps

    def to_blocks(t):
        x_ = t.reshape(B, n, r, t.shape[2], Dh).transpose(0, 2, 1, 3, 4)
        x_ = jnp.pad(x_, ((0, 0), (0, 0), (0, npad - n), (0, 0), (0, 0)))
        return x_.reshape(B, r, nb, steps, t.shape[2], Dh)

    def with_prev(t):
        prev = jnp.pad(t, ((0, 0), (0, 0), (1, 0), (0, 0), (0, 0), (0, 0)))[:, :, :-1]
        return jnp.concatenate([prev, t], axis=3)

    qb = to_blocks(q.astype(F32)).reshape(B, r, nb, steps, kvh, g, Dh)
    kb = with_prev(to_blocks(k.astype(F32)))
    vb = with_prev(to_blocks(v.astype(F32)))
    qi = jnp.arange(nb)[:, None] * steps + jnp.arange(steps)[None, :]
    kj = (jnp.arange(nb)[:, None] - 1) * steps + jnp.arange(2 * steps)[None, :]
    dist = qi[:, :, None] - kj[:, None, :]
    valid = (dist >= 0) & (dist <= steps) & (kj[:, None, :] >= 0)
    bias = -slopes.reshape(kvh, g)[None, :, :, None, None] * (dist * r).astype(F32)[:, None, None]
    s = jnp.einsum('brnqkgd,brnskd->brnkgqs', qb, kb) * (Dh ** -0.5) + bias
    s = jnp.where(valid[:, None, None], s, -jnp.inf)
    m = jnp.max(s, axis=-1, keepdims=True)
    p = jnp.exp(s - m)
    den = jnp.sum(p, axis=-1)
    o = jnp.einsum('brnkgqs,brnskd->brnqkgd', p, vb) / jnp.moveaxis(den, -1, 3)[..., None]
    lse = jnp.moveaxis(m[..., 0] + jnp.log(den), -1, 3)
    o = o.reshape(B, r, npad, H, Dh)[:, :, :n].transpose(0, 2, 1, 3, 4).reshape(B, S, H, Dh)
    lse = lse.reshape(B, r, npad, H)[:, :, :n].transpose(0, 2, 1, 3).reshape(B, S, H)
    return o, lse


def dilated_window_sample(q, k, v, r, steps, slopes, buf_len):
    B, T, H, Dh = q.shape
    kvh = k.shape[2]
    g = H // kvh
    dist = jnp.arange(steps + 1) * r
    idx = buf_len + jnp.arange(T)[:, None] - dist[None, :]
    valid = idx >= 0
    kg = jnp.take(k.astype(F32), jnp.maximum(idx, 0), axis=1)
    vg = jnp.take(v.astype(F32), jnp.maximum(idx, 0), axis=1)
    qg = q.astype(F32).reshape(B, T, kvh, g, Dh)
    bias = -slopes.reshape(kvh, g)[:, :, None] * dist.astype(F32)
    s = jnp.einsum('btkgd,btskd->btkgs', qg, kg) * (Dh ** -0.5) + bias
    s = jnp.where(valid[:, None, None, :], s, -jnp.inf)
    m = jnp.max(s, axis=-1, keepdims=True)
    p = jnp.exp(s - m)
    den = jnp.sum(p, axis=-1)
    o = jnp.einsum('btkgs,btskd->btkgd', p, vg) / den[..., None]
    lse = m[..., 0] + jnp.log(den)
    return o.reshape(B, T, H, Dh), lse.reshape(B, T, H)


def dilated_mixer(xn, branch_kv, w_q, w_out, prompt):
    B, T, _ = xn.shape
    q = (xn @ w_q).reshape(B, T, B_HEADS, B_HEAD_DIM)
    slopes = alibi_slopes(B_HEADS)
    outs, lses = [], []
    for rows, win, r in zip(branch_kv, B_WINDOWS, B_DILATIONS):
        steps = win // r
        if prompt:
            o, lse = dilated_window_prompt(q, rows[:, :, 0], rows[:, :, 1], r, steps, slopes)
        else:
            o, lse = dilated_window_sample(q, rows[:, :, 0], rows[:, :, 1], r, steps, slopes, rows.shape[1] - T)
        outs.append(o)
        lses.append(lse)
    w = jax.nn.softmax(jnp.stack(lses), axis=0)
    o = jnp.einsum('zbth,zbthd->bthd', w, jnp.stack(outs))
    return o.reshape(B, T, B_HEADS * B_HEAD_DIM).astype(xn.dtype) @ w_out


def peer_ffn(xn, wq, keys, u, v):
    B, T, D = xn.shape
    n = B * T
    tb = min(P_TOKEN_BLOCK, n)
    nblk = -(-n // tb)
    xt = jnp.pad(xn.reshape(n, D), ((0, nblk * tb - n), (0, 0)))

    def block(xb):
        q = (xb @ wq).reshape(-1, P_HEADS, 2, P_HALF)
        s = jnp.einsum('thcd,hcnd->thcn', q, keys).astype(F32)
        sv, si = lax.top_k(s, P_TOPK)
        cand = (sv[:, :, 0, :, None] + sv[:, :, 1, None, :]).reshape(-1, P_HEADS, P_TOPK * P_TOPK)
        cidx = (si[:, :, 0, :, None] * P_NKEYS + si[:, :, 1, None, :]).reshape(-1, P_HEADS, P_TOPK * P_TOPK)
        top_s, pos = lax.top_k(cand, P_TOPK)
        eidx = jnp.take_along_axis(cidx, pos, axis=-1).reshape(-1, P_HEADS * P_TOPK)
        gate = jax.nn.softmax(top_s, axis=-1).reshape(-1, P_HEADS * P_TOPK)
        ug = jnp.take(u, eidx, axis=0)
        vg = jnp.take(v, eidx, axis=0)
        act = jax.nn.gelu(jnp.einsum('tkd,td->tk', ug, xb).astype(F32), approximate=False)
        return jnp.einsum('tk,tkd->td', (gate * act).astype(xb.dtype), vg)

    out = lax.map(block, xt.reshape(nblk, tb, D))
    return out.reshape(-1, D)[:n].reshape(B, T, D)


def trunk(x, hgrn_s0, win_caches, prompt, norm_mix_a, w_in_a, lb_logits, gnorm_a, w_out_a, norm_kv, w_kv,
          norm_mix_b, w_q_b, w_out_b, norm_ffn, peer_wq, peer_keys, peer_u, peer_v, norm_final):
    B, T, _ = x.shape
    lb_all = jnp.cumsum(jax.nn.softmax(lb_logits.astype(F32), axis=0), axis=0)
    h = x
    hgrn_states, branch_kv, new_rows = [], [], []
    for layer in range(DEPTH):
        if layer < N_A_LAYERS:
            out, s_fin = hgrn2_mixer(rmsnorm(h, norm_mix_a[layer]), hgrn_s0[:, layer], w_in_a[layer], lb_all[layer],
                                     gnorm_a[layer], w_out_a[layer], prompt)
            hgrn_states.append(s_fin)
        else:
            if layer == N_A_LAYERS:
                kv = (rmsnorm(h, norm_kv) @ w_kv).reshape(B, T, N_BRANCHES, 2, B_KV_HEADS, B_HEAD_DIM)
                for br in range(N_BRANCHES):
                    rows = kv[:, :, br]
                    new_rows.append(rows[:, T - min(B_WINDOWS[br], T):])
                    if win_caches is not None:
                        rows = jnp.concatenate([win_caches[br].astype(rows.dtype), rows], axis=1)
                    branch_kv.append(rows)
            lb_idx = layer - N_A_LAYERS
            out = dilated_mixer(rmsnorm(h, norm_mix_b[lb_idx]), branch_kv, w_q_b[lb_idx], w_out_b[lb_idx], prompt)
        h = h + out.astype(h.dtype)
        h = h + peer_ffn(rmsnorm(h, norm_ffn[layer]), peer_wq[layer], peer_keys[layer], peer_u[layer],
                         peer_v[layer]).astype(h.dtype)
    return rmsnorm(h, norm_final), jnp.stack(hgrn_states, axis=1), new_rows


def setup_inputs(seed: int = 0) -> dict:
    key = jax.random.key(seed)
    ks = jax.random.split(key, 24)

    def nrm(k, shape, scale):
        return jax.random.normal(k, shape, F32) * scale

    def gain(k, shape):
        return 1.0 + 0.02 * jax.random.normal(k, shape, F32)

    kv_cols = N_BRANCHES * 2 * B_KV_HEADS * B_HEAD_DIM
    return {
        'x_prompt': nrm(ks[0], (BATCH, SEQ, D_MODEL), 1.0),
        'x_sample': nrm(ks[1], (DEC_BATCH, DEC_SEQ, D_MODEL), 1.0),
        'state_hgrn': nrm(ks[2], (DEC_BATCH, N_A_LAYERS, A_HEADS, A_DK, A_DV), 0.5),
        'cache_win_r1': nrm(ks[3], (DEC_BATCH, min(B_WINDOWS[0], PAST_LEN), 2, B_KV_HEADS, B_HEAD_DIM), 1.0),
        'cache_win_r4': nrm(ks[4], (DEC_BATCH, min(B_WINDOWS[1], PAST_LEN), 2, B_KV_HEADS, B_HEAD_DIM), 1.0),
        'cache_win_r16': nrm(ks[5], (DEC_BATCH, min(B_WINDOWS[2], PAST_LEN), 2, B_KV_HEADS, B_HEAD_DIM), 1.0),
        'norm_mix_a': gain(ks[6], (N_A_LAYERS, D_MODEL)),
        'w_in_a': nrm(ks[7], (N_A_LAYERS, D_MODEL, 4 * D_MODEL), D_MODEL ** -0.5),
        'lb_logits': nrm(ks[8], (N_A_LAYERS + 1, A_HEADS * A_DK), 0.1),
        'gnorm_a': gain(ks[9], (N_A_LAYERS, A_DV)),
        'w_out_a': nrm(ks[10], (N_A_LAYERS, D_MODEL, D_MODEL), D_MODEL ** -0.5),
        'norm_kv': gain(ks[11], (D_MODEL,)),
        'w_kv': nrm(ks[12], (D_MODEL, kv_cols), D_MODEL ** -0.5),
        'norm_mix_b': gain(ks[13], (N_B_LAYERS, D_MODEL)),
        'w_q_b': nrm(ks[14], (N_B_LAYERS, D_MODEL, B_HEADS * B_HEAD_DIM), D_MODEL ** -0.5),
        'w_out_b': nrm(ks[15], (N_B_LAYERS, B_HEADS * B_HEAD_DIM, D_MODEL), (B_HEADS * B_HEAD_DIM) ** -0.5),
        'norm_ffn': gain(ks[16], (DEPTH, D_MODEL)),
        'peer_wq': nrm(ks[17], (DEPTH, D_MODEL, P_HEADS * P_QDIM), D_MODEL ** -0.5),
        'peer_keys': nrm(ks[18], (DEPTH, P_HEADS, 2, P_NKEYS, P_HALF), P_HALF ** -0.5),
        'peer_u': nrm(ks[19], (DEPTH, P_EXPERTS, D_MODEL), D_MODEL ** -0.5),
        'peer_v': nrm(ks[20], (DEPTH, P_EXPERTS, D_MODEL), P_HEADS ** -0.5),
        'norm_final': gain(ks[21], (D_MODEL,)),
    }


def reference(x_prompt, x_sample, state_hgrn, cache_win_r1, cache_win_r4, cache_win_r16, norm_mix_a, w_in_a,
              lb_logits, gnorm_a, w_out_a, norm_kv, w_kv, norm_mix_b, w_q_b, w_out_b, norm_ffn, peer_wq, peer_keys,
              peer_u, peer_v, norm_final):
    params = (norm_mix_a, w_in_a, lb_logits, gnorm_a, w_out_a, norm_kv, w_kv, norm_mix_b, w_q_b, w_out_b,
              norm_ffn, peer_wq, peer_keys, peer_u, peer_v, norm_final)
    s0_prompt = jnp.zeros((x_prompt.shape[0], N_A_LAYERS, A_HEADS, A_DK, A_DV), F32)
    y_prompt, hgrn_p, rows_p = trunk(x_prompt, s0_prompt, None, True, *params)
    y_sample, hgrn_s, rows_s = trunk(x_sample, state_hgrn, (cache_win_r1, cache_win_r4, cache_win_r16), False,
                                     *params)
    return (y_prompt, y_sample, hgrn_p, rows_p[0], rows_p[1], rows_p[2], hgrn_s, rows_s[0], rows_s[1], rows_s[2])
```

```python
import functools
import math

import jax
import jax.numpy as jnp
from jax import lax
from jax.experimental import pallas as pl
from jax.experimental.pallas import tpu as pltpu

F32 = jnp.float32
BF16 = jnp.bfloat16
EPS = 1e-6

LANES = 128
SUBLANES = 8
VMEM_LIMIT = 56 * 1024 * 1024

A_HEADS = 16
A_DK = 128
A_CHUNK = 32
B_HEADS = 16
B_HEAD_DIM = 128
B_KV_HEADS = 4
B_WINDOWS = (128, 512, 2048)
B_DILATIONS = (1, 4, 16)
N_BRANCHES = 3
P_HEADS = 8
P_NKEYS = 128
P_TOPK = 16
P_HALF = 128
NEG_INF = float("-inf")


def _params(*sem):
    return pltpu.CompilerParams(dimension_semantics=sem, vmem_limit_bytes=VMEM_LIMIT)


def _dot(a, b):
    return jnp.dot(a, b, preferred_element_type=F32)


def _dot_nt(a, b):
    return lax.dot_general(a, b, (((1,), (1,)), ((), ())), preferred_element_type=F32)


def _dot_tn(a, b):
    return lax.dot_general(a, b, (((0,), (0,)), ((), ())), preferred_element_type=F32)


def _rms(x, g):
    ms = jnp.mean(x * x, axis=-1, keepdims=True)
    return x * lax.rsqrt(ms + EPS) * g


def _row_tile(m, want):
    t = min(m, want)
    while m % t:
        t //= 2
    return t


def _norm_matmul_kernel(x_ref, g_ref, w_ref, o_ref, xn_ref):
    @pl.when(pl.program_id(1) == 0)
    def _():
        xn_ref[...] = _rms(x_ref[...], g_ref[...]).astype(BF16)

    o_ref[...] = _dot(xn_ref[...], w_ref[...])


def norm_matmul(x, g, w):
    m, k = x.shape
    n = w.shape[1]
    tm = _row_tile(m, 512)
    tn = _row_tile(n, 512)
    return pl.pallas_call(
        _norm_matmul_kernel,
        grid=(m // tm, n // tn),
        in_specs=[pl.BlockSpec((tm, k), lambda i, j: (i, 0)),
                  pl.BlockSpec((1, k), lambda i, j: (0, 0)),
                  pl.BlockSpec((k, tn), lambda i, j: (0, j))],
        out_specs=pl.BlockSpec((tm, tn), lambda i, j: (i, j)),
        out_shape=jax.ShapeDtypeStruct((m, n), F32),
        scratch_shapes=[pltpu.VMEM((tm, k), BF16)],
        compiler_params=_params("parallel", "arbitrary"),
        name="norm_matmul",
    )(x, g.reshape(1, k), w)


def _matmul_res_kernel(a_ref, w_ref, r_ref, o_ref):
    o_ref[...] = r_ref[...] + _dot(a_ref[...], w_ref[...])


def matmul_residual(a, w, res):
    m, k = a.shape
    n = w.shape[1]
    tm = _row_tile(m, 512)
    tn = _row_tile(n, 512)
    return pl.pallas_call(
        _matmul_res_kernel,
        grid=(m // tm, n // tn),
        in_specs=[pl.BlockSpec((tm, k), lambda i, j: (i, 0)),
                  pl.BlockSpec((k, tn), lambda i, j: (0, j)),
                  pl.BlockSpec((tm, tn), lambda i, j: (i, j))],
        out_specs=pl.BlockSpec((tm, tn), lambda i, j: (i, j)),
        out_shape=jax.ShapeDtypeStruct((m, n), F32),
        compiler_params=_params("parallel", "arbitrary"),
        name="matmul_residual",
    )(a, w, res)


def _split3(x):
    hi = x.astype(BF16)
    r1 = x - hi.astype(F32)
    mid = r1.astype(BF16)
    lo = (r1 - mid.astype(F32)).astype(BF16)
    return hi, mid, lo


def _gated_head_norm(o, gn, g):
    return _rms(o, gn) * jax.nn.sigmoid(g)


def _hgrn_prompt_kernel(q_ref, f_ref, i_ref, g_ref, lb_ref, gn_ref, o_ref, s_ref):
    seq = q_ref.shape[0]
    grp = LANES
    row = lax.broadcasted_iota(jnp.int32, (grp, grp), 0)
    col = lax.broadcasted_iota(jnp.int32, (grp, grp), 1)
    shift = A_CHUNK.bit_length() - 1
    same = jnp.right_shift(row, shift) == jnp.right_shift(col, shift)
    tri = same & (col <= row)
    l_tri = tri.astype(BF16)
    l_all = same.astype(BF16)
    lb = lb_ref[...]
    gn = gn_ref[...]

    def group(gi, st):
        sl = pl.ds(pl.multiple_of(gi * grp, grp), grp)
        q = q_ref[sl, :]
        v = i_ref[sl, :]
        f = lb + (1.0 - lb) * jax.nn.sigmoid(f_ref[sl, :])
        k = 1.0 - f
        hi, mid, lo = _split3(jnp.log(f))
        b = _dot(l_tri, hi) + _dot(l_tri, mid) + _dot(l_tri, lo)
        bt = _dot(l_all, hi) + _dot(l_all, mid) + _dot(l_all, lo)
        qe = (q * jnp.exp(b)).astype(BF16)
        ke = (k * jnp.exp(-b)).astype(BF16)
        kd = (k * jnp.exp(bt - b)).astype(BF16)
        vb = v.astype(BF16)
        att = jnp.where(tri, _dot_nt(qe, ke), 0.0)
        o_intra = _dot(att.astype(BF16), vb)
        dec_t = jnp.exp(bt).T
        outs = []
        for c in range(grp // A_CHUNK):
            rows = slice(c * A_CHUNK, (c + 1) * A_CHUNK)
            outs.append(o_intra[rows] + _dot(qe[rows], st.astype(BF16)))
            st = dec_t[:, c * A_CHUNK:c * A_CHUNK + 1] * st + _dot_tn(kd[rows], vb[rows])
        o = jnp.concatenate(outs, axis=0)
        o_ref[sl, :] = _gated_head_norm(o, gn, g_ref[sl, :]).astype(BF16)
        return st

    st = lax.fori_loop(0, seq // grp, group, jnp.zeros((A_DK, LANES), F32))
    s_ref[...] = st


def hgrn_prompt(proj, lb, gnorm, batch, seq):
    d = proj.shape[1] // 4
    dv = d // A_HEADS
    nh = A_HEADS

    def col(off):
        return pl.BlockSpec((seq, dv), lambda b, h: (b, off * nh + h))

    return pl.pallas_call(
        _hgrn_prompt_kernel,
        grid=(batch, nh),
        in_specs=[col(0), col(1), col(2), col(3),
                  pl.BlockSpec((1, A_DK), lambda b, h: (0, h)),
                  pl.BlockSpec((1, dv), lambda b, h: (0, 0))],
        out_specs=[pl.BlockSpec((seq, dv), lambda b, h: (b, h)),
                   pl.BlockSpec((None, None, A_DK, dv), lambda b, h: (b, h, 0, 0))],
        out_shape=[jax.ShapeDtypeStruct((batch * seq, d), BF16),
                   jax.ShapeDtypeStruct((batch, nh, A_DK, dv), F32)],
        compiler_params=_params("parallel", "parallel"),
        name="hgrn_prompt",
    )(proj, proj, proj, proj, lb.reshape(1, -1), gnorm.reshape(1, -1))


def _as_column(rowvec):
    n = rowvec.shape[1]
    return jnp.broadcast_to(rowvec, (n, n)).T


def _hgrn_step_kernel(q_ref, f_ref, i_ref, g_ref, lb_ref, gn_ref, s_ref, o_ref, so_ref):
    f = lb_ref[...] + (1.0 - lb_ref[...]) * jax.nn.sigmoid(f_ref[...])
    s_new = _as_column(f) * s_ref[...] + _as_column(1.0 - f) * i_ref[...]
    so_ref[...] = s_new
    o = jnp.sum(_as_column(q_ref[...]) * s_new, axis=0, keepdims=True)
    o_ref[...] = _gated_head_norm(o, gn_ref[...], g_ref[...])


def hgrn_step(proj, lb, gnorm, state):
    bsz = proj.shape[0]
    d = proj.shape[1] // 4
    dv = d // A_HEADS
    nh = A_HEADS
    proj3 = proj.reshape(bsz, 1, 4 * d)

    def col(off):
        return pl.BlockSpec((None, 1, dv), lambda b, h: (b, 0, off * nh + h))

    o, s_new = pl.pallas_call(
        _hgrn_step_kernel,
        grid=(bsz, nh),
        in_specs=[col(0), col(1), col(2), col(3),
                  pl.BlockSpec((1, A_DK), lambda b, h: (0, h)),
                  pl.BlockSpec((1, dv), lambda b, h: (0, 0)),
                  pl.BlockSpec((None, None, A_DK, dv), lambda b, h: (b, h, 0, 0))],
        out_specs=[pl.BlockSpec((None, 1, dv), lambda b, h: (b, 0, h)),
                   pl.BlockSpec((None, None, A_DK, dv), lambda b, h: (b, h, 0, 0))],
        out_shape=[jax.ShapeDtypeStruct((bsz, 1, d), F32),
                   jax.ShapeDtypeStruct(state.shape, F32)],
        compiler_params=_params("parallel", "parallel"),
        name="hgrn_step",
    )(proj3, proj3, proj3, proj3, lb.reshape(1, -1), gnorm.reshape(1, -1), state)
    return o.reshape(bsz, d).astype(BF16), s_new


def _alibi_slope(h):
    return 2.0 ** (-8.0 * (h + 1) / B_HEADS)


def _dilated_prompt_kernel(q_ref, kp_ref, vp_ref, kc_ref, vc_ref, o_ref, lse_ref, *, dilation, steps):
    bq = q_ref.shape[0]
    blk = pl.program_id(2)
    qa = lax.broadcasted_iota(jnp.int32, (bq, 2 * bq), 0)
    kb = lax.broadcasted_iota(jnp.int32, (bq, 2 * bq), 1)
    dist = bq + qa - kb
    valid = (dist >= 0) & (dist <= steps) & ((kb >= bq) | (blk > 0))
    distf = (dist * dilation).astype(F32)
    lane = lax.broadcasted_iota(jnp.int32, (bq, LANES), 1)
    scale = B_HEAD_DIM ** -0.5
    group = B_HEADS // B_KV_HEADS
    lse_tile = jnp.zeros((bq, LANES), F32)
    for kh in range(B_KV_HEADS):
        cs = slice(kh * B_HEAD_DIM, (kh + 1) * B_HEAD_DIM)
        keys = jnp.concatenate([kp_ref[:, cs], kc_ref[:, cs]], axis=0).astype(BF16)
        vals = jnp.concatenate([vp_ref[:, cs], vc_ref[:, cs]], axis=0).astype(BF16)
        for gq in range(group):
            h = kh * group + gq
            hs = slice(h * B_HEAD_DIM, (h + 1) * B_HEAD_DIM)
            s = _dot_nt(q_ref[:, hs].astype(BF16), keys) * scale - _alibi_slope(h) * distf
            s = jnp.where(valid, s, NEG_INF)
            m = jnp.max(s, axis=-1, keepdims=True)
            p = jnp.exp(s - m)
            den = jnp.sum(p, axis=-1, keepdims=True)
            o_ref[:, hs] = _dot(p.astype(BF16), vals) / den
            lse_tile = jnp.where(lane == h, m + jnp.log(den), lse_tile)
    lse_ref[...] = lse_tile


def dilated_prompt(qp, kv, branch, batch, seq):
    dil = B_DILATIONS[branch]
    steps = B_WINDOWS[branch] // dil
    n = seq // dil
    bq = min(steps, n)
    hd = B_HEADS * B_HEAD_DIM
    kw = B_KV_HEADS * B_HEAD_DIM
    q3 = qp.reshape(batch, n, dil * hd)
    kv3 = kv.reshape(batch, n, dil * N_BRANCHES * 2 * kw)
    kcol = 2 * branch

    def kvspec(off, prev):
        if prev:
            return pl.BlockSpec((None, bq, kw),
                                lambda b, c, i: (b, jnp.maximum(i - 1, 0), c * 2 * N_BRANCHES + kcol + off))
        return pl.BlockSpec((None, bq, kw), lambda b, c, i: (b, i, c * 2 * N_BRANCHES + kcol + off))

    o, lse = pl.pallas_call(
        functools.partial(_dilated_prompt_kernel, dilation=dil, steps=steps),
        grid=(batch, dil, n // bq),
        in_specs=[pl.BlockSpec((None, bq, hd), lambda b, c, i: (b, i, c)),
                  kvspec(0, True), kvspec(1, True), kvspec(0, False), kvspec(1, False)],
        out_specs=[pl.BlockSpec((None, bq, hd), lambda b, c, i: (b, i, c)),
                   pl.BlockSpec((None, bq, LANES), lambda b, c, i: (b, i, c))],
        out_shape=[jax.ShapeDtypeStruct((batch, n, dil * hd), F32),
                   jax.ShapeDtypeStruct((batch, n, dil * LANES), F32)],
        compiler_params=_params("parallel", "parallel", "arbitrary"),
        name=f"dilated_prompt_r{dil}",
    )(q3, kv3, kv3, kv3, kv3)
    return o.reshape(batch * seq, hd), lse.reshape(batch * seq, LANES)


def _branch_mix_kernel(o0_ref, o1_ref, o2_ref, l0_ref, l1_ref, l2_ref, out_ref):
    l0, l1, l2 = l0_ref[...], l1_ref[...], l2_ref[...]
    m = jnp.maximum(jnp.maximum(l0, l1), l2)
    e0, e1, e2 = jnp.exp(l0 - m), jnp.exp(l1 - m), jnp.exp(l2 - m)
    tot = e0 + e1 + e2
    w0, w1, w2 = e0 / tot, e1 / tot, e2 / tot
    for h in range(B_HEADS):
        hs = slice(h * B_HEAD_DIM, (h + 1) * B_HEAD_DIM)
        mix = (w0[:, h:h + 1] * o0_ref[:, hs] + w1[:, h:h + 1] * o1_ref[:, hs]
               + w2[:, h:h + 1] * o2_ref[:, hs])
        out_ref[:, hs] = mix.astype(BF16)


def branch_mix(outs, lses):
    m, hd = outs[0].shape
    tm = _row_tile(m, 256)
    ospec = pl.BlockSpec((tm, hd), lambda i: (i, 0))
    lspec = pl.BlockSpec((tm, LANES), lambda i: (i, 0))
    return pl.pallas_call(
        _branch_mix_kernel,
        grid=(m // tm,),
        in_specs=[ospec] * 3 + [lspec] * 3,
        out_specs=ospec,
        out_shape=jax.ShapeDtypeStruct((m, hd), BF16),
        compiler_params=_params("parallel"),
        name="branch_mix",
    )(*outs, *lses)


def _dilated_sample_kernel(q_ref, new_ref, c0_ref, c1_ref, c2_ref, o_ref):
    caches = (c0_ref, c1_ref, c2_ref)
    nrow = c0_ref.shape[0]
    kw = B_KV_HEADS * B_HEAD_DIM
    scale = B_HEAD_DIM ** -0.5
    group = B_HEADS // B_KV_HEADS
    rowi = lax.broadcasted_iota(jnp.int32, (nrow, 1), 0)
    for h in range(B_HEADS):
        kh = h // group
        qh = q_ref[:, h * B_HEAD_DIM:(h + 1) * B_HEAD_DIM]
        outs, lses = [], []
        for z in range(N_BRANCHES):
            kc = caches[z][:, kh * B_HEAD_DIM:(kh + 1) * B_HEAD_DIM]
            vc = caches[z][:, kw + kh * B_HEAD_DIM:kw + (kh + 1) * B_HEAD_DIM]
            base = z * 2 * kw
            kn = new_ref[:, base + kh * B_HEAD_DIM:base + (kh + 1) * B_HEAD_DIM]
            vn = new_ref[:, base + kw + kh * B_HEAD_DIM:base + kw + (kh + 1) * B_HEAD_DIM]
            dist = ((nrow - rowi) * B_DILATIONS[z]).astype(F32)
            sc = jnp.sum(kc * qh, axis=-1, keepdims=True) * scale - _alibi_slope(h) * dist
            sn = jnp.sum(kn * qh, axis=-1, keepdims=True) * scale
            m = jnp.maximum(jnp.max(sc, axis=0, keepdims=True), sn)
            pc = jnp.exp(sc - m)
            pn = jnp.exp(sn - m)
            den = jnp.sum(pc, axis=0, keepdims=True) + pn
            outs.append((jnp.sum(pc * vc, axis=0, keepdims=True) + pn * vn) / den)
            lses.append(m + jnp.log(den))
        mm = jnp.maximum(jnp.maximum(lses[0], lses[1]), lses[2])
        es = [jnp.exp(l - mm) for l in lses]
        tot = es[0] + es[1] + es[2]
        mix = (es[0] / tot) * outs[0] + (es[1] / tot) * outs[1] + (es[2] / tot) * outs[2]
        o_ref[:, h * B_HEAD_DIM:(h + 1) * B_HEAD_DIM] = mix


def dilated_sample(qs, kv_new, caches):
    bsz, hd = qs.shape
    kw2 = 2 * B_KV_HEADS * B_HEAD_DIM
    in_specs = [pl.BlockSpec((None, 1, hd), lambda b: (b, 0, 0)),
                pl.BlockSpec((None, 1, kv_new.shape[1]), lambda b: (b, 0, 0))]
    args = [qs.reshape(bsz, 1, hd), kv_new.reshape(bsz, 1, -1)]
    for z, cache in enumerate(caches):
        dil = B_DILATIONS[z]
        steps = B_WINDOWS[z] // dil
        assert cache.shape[1] == steps * dil
        args.append(cache.reshape(bsz, steps, dil * kw2))
        in_specs.append(pl.BlockSpec((None, steps, kw2), lambda b: (b, 0, 0)))
    out = pl.pallas_call(
        _dilated_sample_kernel,
        grid=(bsz,),
        in_specs=in_specs,
        out_specs=pl.BlockSpec((None, 1, hd), lambda b: (b, 0, 0)),
        out_shape=jax.ShapeDtypeStruct((bsz, 1, hd), F32),
        compiler_params=_params("parallel"),
        name="dilated_sample",
    )(*args)
    return out.reshape(bsz, hd).astype(BF16)


def _topk_rows(x, payload=None):
    r = x.shape[0]
    iota = lax.broadcasted_iota(jnp.int32, x.shape, 0).astype(F32)
    vals, idxs = [], []
    for _ in range(P_TOPK):
        m = jnp.max(x, axis=0, keepdims=True)
        am = jnp.min(jnp.where(x == m, iota, float(r)), axis=0, keepdims=True)
        hit = iota == am
        vals.append(m)
        if payload is None:
            idxs.append(am)
        else:
            idxs.append(jnp.max(jnp.where(hit, payload, 0.0), axis=0, keepdims=True))
        x = jnp.where(hit, NEG_INF, x)
    return jnp.concatenate(vals, axis=0), jnp.concatenate(idxs, axis=0)


def _peer_route_kernel(q_ref, keys_ref, idx_ref, gate_ref):
    idx_rows, gate_rows = [], []
    for h in range(P_HEADS):
        sub = []
        for c in range(2):
            off = (h * 2 + c) * P_HALF
            qhc = q_ref[:, off:off + P_HALF].astype(BF16)
            s_t = _dot_nt(keys_ref[h, c].astype(BF16), qhc)
            sub.append(_topk_rows(s_t))
        (sv0, si0), (sv1, si1) = sub
        cand = jnp.concatenate([sv0[a:a + 1] + sv1 for a in range(P_TOPK)], axis=0)
        cidx = jnp.concatenate([si0[a:a + 1] * P_NKEYS + si1 for a in range(P_TOPK)], axis=0)
        top_s, eidx = _topk_rows(cand, cidx)
        e = jnp.exp(top_s - top_s[0:1])
        gate_rows.append(e / jnp.sum(e, axis=0, keepdims=True))
        idx_rows.append(eidx)
    idx_ref[...] = jnp.concatenate(idx_rows, axis=0).T.astype(jnp.int32)
    gate_ref[...] = jnp.concatenate(gate_rows, axis=0).T


def peer_route(q, keys):
    t, qd = q.shape
    tt = _row_tile(t, LANES)
    nsel = P_HEADS * P_TOPK
    return pl.pallas_call(
        _peer_route_kernel,
        grid=(t // tt,),
        in_specs=[pl.BlockSpec((tt, qd), lambda i: (i, 0)),
                  pl.BlockSpec(keys.shape, lambda i: (0, 0, 0, 0))],
        out_specs=[pl.BlockSpec((tt, nsel), lambda i: (i, 0)),
                   pl.BlockSpec((tt, nsel), lambda i: (i, 0))],
        out_shape=[jax.ShapeDtypeStruct((t, nsel), jnp.int32),
                   jax.ShapeDtypeStruct((t, nsel), F32)],
        compiler_params=_params("parallel"),
        name="peer_route",
    )(q, keys)


def _gelu_exact(x):
    return 0.5 * x * (1.0 + lax.erf(x * (1.0 / math.sqrt(2.0))))


def _peer_expert_kernel(idx_hbm, gate_ref, h_ref, g_ref, fg_ref, uv_hbm, out_ref,
                        idx_smem, buf, xn_ref, gcol_ref, idx_sem, row_sems, *, final_norm):
    tt, d = h_ref.shape
    nsel = gate_ref.shape[1]
    step = pl.program_id(0)

    idx_copy = pltpu.make_async_copy(idx_hbm.at[pl.ds(step * tt, tt)], idx_smem, idx_sem)
    idx_copy.start()
    xn_ref[...] = _rms(h_ref[...], g_ref[...])
    gate_t = gate_ref[...].T
    for t in range(tt):
        gcol_ref[t] = gate_t[:, t:t + 1]
    idx_copy.wait()

    def issue(t, slot):
        for k in range(nsel):
            pltpu.make_async_copy(uv_hbm.at[idx_smem[t, k]], buf.at[slot, k], row_sems.at[slot]).start()

    def wait_rows(slot):
        pltpu.make_async_copy(uv_hbm.at[pl.ds(0, nsel)], buf.at[slot], row_sems.at[slot]).wait()

    issue(0, 0)

    def token(t, carry):
        slot = lax.rem(t, 2)

        @pl.when(t + 1 < tt)
        def _():
            issue(t + 1, 1 - slot)

        wait_rows(slot)
        x = xn_ref[pl.ds(t, 1), :]
        acc = jnp.zeros((nsel, LANES), F32)
        for j in range(d // LANES):
            cs = slice(j * LANES, (j + 1) * LANES)
            acc = acc + buf[slot, :, cs] * x[:, cs]
        act = jnp.sum(acc, axis=-1, keepdims=True)
        w = jnp.broadcast_to(gcol_ref[t] * _gelu_exact(act), (nsel, LANES))
        pieces = []
        for j in range(d // LANES):
            cs = slice(d + j * LANES, d + (j + 1) * LANES)
            pieces.append(jnp.sum(buf[slot, :, cs] * w, axis=0, keepdims=True))
        out_ref[pl.ds(t, 1), :] = h_ref[pl.ds(t, 1), :] + jnp.concatenate(pieces, axis=1)
        return carry

    lax.fori_loop(0, tt, token, 0)
    if final_norm:
        out_ref[...] = _rms(out_ref[...], fg_ref[...])


def peer_experts(h, gain, idx, gate, uv, final_gain=None):
    t, d = h.shape
    nsel = idx.shape[1]
    tt = _row_tile(t, 32)
    final_norm = final_gain is not None
    fg = (final_gain if final_norm else gain).reshape(1, d)
    return pl.pallas_call(
        functools.partial(_peer_expert_kernel, final_norm=final_norm),
        grid=(t // tt,),
        in_specs=[pl.BlockSpec(memory_space=pl.ANY),
                  pl.BlockSpec((tt, nsel), lambda i: (i, 0)),
                  pl.BlockSpec((tt, d), lambda i: (i, 0)),
                  pl.BlockSpec((1, d), lambda i: (0, 0)),
                  pl.BlockSpec((1, d), lambda i: (0, 0)),
                  pl.BlockSpec(memory_space=pl.ANY)],
        out_specs=pl.BlockSpec((tt, d), lambda i: (i, 0)),
        out_shape=jax.ShapeDtypeStruct((t, d), F32),
        scratch_shapes=[pltpu.SMEM((tt, nsel), jnp.int32),
                        pltpu.VMEM((2, nsel, 2 * d), F32),
                        pltpu.VMEM((tt, d), F32),
                        pltpu.VMEM((tt, nsel, 1), F32),
                        pltpu.SemaphoreType.DMA(()),
                        pltpu.SemaphoreType.DMA((2,))],
        compiler_params=_params("arbitrary"),
        name="peer_experts",
    )(idx, gate, h, gain.reshape(1, d), fg, uv)


def peer_layer(h, gain, wq, keys, uv, final_gain=None):
    q = norm_matmul(h, gain, wq)
    idx, gate = peer_route(q, keys)
    return peer_experts(h, gain, idx, gate, uv, final_gain)


def kernel(x_prompt, x_sample, state_hgrn, cache_win_r1, cache_win_r4, cache_win_r16, norm_mix_a, w_in_a, lb_logits, gnorm_a, w_out_a, norm_kv, w_kv, norm_mix_b, w_q_b, w_out_b, norm_ffn, peer_wq, peer_keys, peer_u, peer_v, norm_final):
    batch, seq, d = x_prompt.shape
    dec = x_sample.shape[0]
    n_a = w_in_a.shape[0]
    depth = norm_ffn.shape[0]
    caches = (cache_win_r1, cache_win_r4, cache_win_r16)
    lb_all = jnp.cumsum(jax.nn.softmax(lb_logits.astype(F32), axis=0), axis=0)
    kw2 = 2 * B_KV_HEADS * B_HEAD_DIM

    hp = x_prompt.reshape(batch * seq, d)
    hs = x_sample.reshape(dec, d)
    states_p, states_s = [], []
    kv_p = kv_s = None
    for layer in range(depth):
        if layer < n_a:
            w_in = w_in_a[layer].astype(BF16)
            w_out = w_out_a[layer].astype(BF16)
            proj_p = norm_matmul(hp, norm_mix_a[layer], w_in)
            o_p, st_p = hgrn_prompt(proj_p, lb_all[layer], gnorm_a[layer], batch, seq)
            hp = matmul_residual(o_p, w_out, hp)
            proj_s = norm_matmul(hs, norm_mix_a[layer], w_in)
            o_s, st_s = hgrn_step(proj_s, lb_all[layer], gnorm_a[layer], state_hgrn[:, layer])
            hs = matmul_residual(o_s, w_out, hs)
            states_p.append(st_p)
            states_s.append(st_s)
        else:
            li = layer - n_a
            if layer == n_a:
                wkv = w_kv.astype(BF16)
                kv_p = norm_matmul(hp, norm_kv, wkv)
                kv_s = norm_matmul(hs, norm_kv, wkv)
            wq = w_q_b[li].astype(BF16)
            w_out = w_out_b[li].astype(BF16)
            q_p = norm_matmul(hp, norm_mix_b[li], wq)
            parts = [dilated_prompt(q_p, kv_p, z, batch, seq) for z in range(N_BRANCHES)]
            mix_p = branch_mix([p[0] for p in parts], [p[1] for p in parts])
            hp = matmul_residual(mix_p, w_out, hp)
            q_s = norm_matmul(hs, norm_mix_b[li], wq)
            mix_s = dilated_sample(q_s, kv_s, caches)
            hs = matmul_residual(mix_s, w_out, hs)
        uv = jnp.concatenate([peer_u[layer], peer_v[layer]], axis=1)
        pwq = peer_wq[layer].astype(BF16)
        fin = norm_final if layer == depth - 1 else None
        hp = peer_layer(hp, norm_ffn[layer], pwq, peer_keys[layer], uv, fin)
        hs = peer_layer(hs, norm_ffn[layer], pwq, peer_keys[layer], uv, fin)

    kv_p6 = kv_p.reshape(batch, seq, N_BRANCHES, 2, B_KV_HEADS, B_HEAD_DIM)
    kv_s6 = kv_s.reshape(dec, 1, N_BRANCHES, 2, B_KV_HEADS, B_HEAD_DIM)
    rows_p = [kv_p6[:, seq - min(B_WINDOWS[z], seq):, z] for z in range(N_BRANCHES)]
    rows_s = [kv_s6[:, :, z] for z in range(N_BRANCHES)]
    return (hp.reshape(batch, seq, d), hs.reshape(dec, 1, d),
            jnp.stack(states_p, axis=1), rows_p[0], rows_p[1], rows_p[2],
            jnp.stack(states_s, axis=1), rows_s[0], rows_s[1], rows_s[2])
```

```python
import functools
import math

import jax
import jax.numpy as jnp
from jax import lax
from jax.experimental import pallas as pl
from jax.experimental.pallas import tpu as pltpu

F32 = jnp.float32
BF16 = jnp.bfloat16
EPS = 1e-6

LANES = 128
SUBLANES = 8
VMEM_LIMIT = 56 * 1024 * 1024

A_HEADS = 16
A_DK = 128
A_CHUNK = 32
B_HEADS = 16
B_HEAD_DIM = 128
B_KV_HEADS = 4
B_WINDOWS = (128, 512, 2048)
B_DILATIONS = (1, 4, 16)
N_BRANCHES = 3
P_HEADS = 8
P_NKEYS = 128
P_TOPK = 16
P_HALF = 128
NEG_INF = float("-inf")


def _params(*sem):
    return pltpu.CompilerParams(dimension_semantics=sem, vmem_limit_bytes=VMEM_LIMIT)


def _dot(a, b):
    return jnp.dot(a, b, preferred_element_type=F32)


def _dot_nt(a, b):
    return lax.dot_general(a, b, (((1,), (1,)), ((), ())), preferred_element_type=F32)


def _dot_tn(a, b):
    return lax.dot_general(a, b, (((0,), (0,)), ((), ())), preferred_element_type=F32)


def _rms(x, g):
    ms = jnp.mean(x * x, axis=-1, keepdims=True)
    return x * lax.rsqrt(ms + EPS) * g


def _row_tile(m, want):
    t = min(m, want)
    while m % t:
        t //= 2
    return t


def _norm_matmul_kernel(x_ref, g_ref, w_ref, o_ref, xn_ref):
    @pl.when(pl.program_id(1) == 0)
    def _():
        xn_ref[...] = _rms(x_ref[...], g_ref[...]).astype(BF16)

    o_ref[...] = _dot(xn_ref[...], w_ref[...])


def norm_matmul(x, g, w):
    m, k = x.shape
    n = w.shape[1]
    tm = _row_tile(m, 512)
    tn = _row_tile(n, 512)
    return pl.pallas_call(
        _norm_matmul_kernel,
        grid=(m // tm, n // tn),
        in_specs=[pl.BlockSpec((tm, k), lambda i, j: (i, 0)),
                  pl.BlockSpec((1, k), lambda i, j: (0, 0)),
                  pl.BlockSpec((k, tn), lambda i, j: (0, j))],
        out_specs=pl.BlockSpec((tm, tn), lambda i, j: (i, j)),
        out_shape=jax.ShapeDtypeStruct((m, n), F32),
        scratch_shapes=[pltpu.VMEM((tm, k), BF16)],
        compiler_params=_params("parallel", "arbitrary"),
        name="norm_matmul",
    )(x, g.reshape(1, k), w)


def _matmul_res_kernel(a_ref, w_ref, r_ref, o_ref):
    o_ref[...] = r_ref[...] + _dot(a_ref[...], w_ref[...])


def matmul_residual(a, w, res):
    m, k = a.shape
    n = w.shape[1]
    tm = _row_tile(m, 512)
    tn = _row_tile(n, 512)
    return pl.pallas_call(
        _matmul_res_kernel,
        grid=(m // tm, n // tn),
        in_specs=[pl.BlockSpec((tm, k), lambda i, j: (i, 0)),
                  pl.BlockSpec((k, tn), lambda i, j: (0, j)),
                  pl.BlockSpec((tm, tn), lambda i, j: (i, j))],
        out_specs=pl.BlockSpec((tm, tn), lambda i, j: (i, j)),
        out_shape=jax.ShapeDtypeStruct((m, n), F32),
        compiler_params=_params("parallel", "arbitrary"),
        name="matmul_residual",
    )(a, w, res)


def _split3(x):
    hi = x.astype(BF16)
    r1 = x - hi.astype(F32)
    mid = r1.astype(BF16)
    lo = (r1 - mid.astype(F32)).astype(BF16)
    return hi, mid, lo


def _gated_head_norm(o, gn, g):
    return _rms(o, gn) * jax.nn.sigmoid(g)


def _hgrn_prompt_kernel(q_ref, f_ref, i_ref, g_ref, lb_ref, gn_ref, o_ref, s_ref):
    seq = q_ref.shape[0]
    grp = LANES
    row = lax.broadcasted_iota(jnp.int32, (grp, grp), 0)
    col = lax.broadcasted_iota(jnp.int32, (grp, grp), 1)
    shift = A_CHUNK.bit_length() - 1
    same = jnp.right_shift(row, shift) == jnp.right_shift(col, shift)
    tri = same & (col <= row)
    l_tri = tri.astype(BF16)
    l_all = same.astype(BF16)
    lb = lb_ref[...]
    gn = gn_ref[...]

    def group(gi, st):
        sl = pl.ds(pl.multiple_of(gi * grp, grp), grp)
        q = q_ref[sl, :]
        v = i_ref[sl, :]
        f = lb + (1.0 - lb) * jax.nn.sigmoid(f_ref[sl, :])
        k = 1.0 - f
        hi, mid, lo = _split3(jnp.log(f))
        b = _dot(l_tri, hi) + _dot(l_tri, mid) + _dot(l_tri, lo)
        bt = _dot(l_all, hi) + _dot(l_all, mid) + _dot(l_all, lo)
        qe = (q * jnp.exp(b)).astype(BF16)
        ke = (k * jnp.exp(-b)).astype(BF16)
        kd = (k * jnp.exp(bt - b)).astype(BF16)
        vb = v.astype(BF16)
        att = jnp.where(tri, _dot_nt(qe, ke), 0.0)
        o_intra = _dot(att.astype(BF16), vb)
        dec_t = jnp.exp(bt).T
        outs = []
        for c in range(grp // A_CHUNK):
            rows = slice(c * A_CHUNK, (c + 1) * A_CHUNK)
            outs.append(o_intra[rows] + _dot(qe[rows], st.astype(BF16)))
            st = dec_t[:, c * A_CHUNK:c * A_CHUNK + 1] * st + _dot_tn(kd[rows], vb[rows])
        o = jnp.concatenate(outs, axis=0)
        o_ref[sl, :] = _gated_head_norm(o, gn, g_ref[sl, :]).astype(BF16)
        return st

    st = lax.fori_loop(0, seq // grp, group, jnp.zeros((A_DK, LANES), F32))
    s_ref[...] = st


def hgrn_prompt(proj, lb, gnorm, batch, seq):
    d = proj.shape[1] // 4
    dv = d // A_HEADS
    nh = A_HEADS

    def col(off):
        return pl.BlockSpec((seq, dv), lambda b, h: (b, off * nh + h))

    return pl.pallas_call(
        _hgrn_prompt_kernel,
        grid=(batch, nh),
        in_specs=[col(0), col(1), col(2), col(3),
                  pl.BlockSpec((1, A_DK), lambda b, h: (0, h)),
                  pl.BlockSpec((1, dv), lambda b, h: (0, 0))],
        out_specs=[pl.BlockSpec((seq, dv), lambda b, h: (b, h)),
                   pl.BlockSpec((None, None, A_DK, dv), lambda b, h: (b, h, 0, 0))],
        out_shape=[jax.ShapeDtypeStruct((batch * seq, d), BF16),
                   jax.ShapeDtypeStruct((batch, nh, A_DK, dv), F32)],
        compiler_params=_params("parallel", "parallel"),
        name="hgrn_prompt",
    )(proj, proj, proj, proj, lb.reshape(1, -1), gnorm.reshape(1, -1))


def _as_column(rowvec):
    n = rowvec.shape[1]
    return jnp.broadcast_to(rowvec, (n, n)).T


def _hgrn_step_kernel(q_ref, f_ref, i_ref, g_ref, lb_ref, gn_ref, s_ref, o_ref, so_ref):
    f = lb_ref[...] + (1.0 - lb_ref[...]) * jax.nn.sigmoid(f_ref[...])
    s_new = _as_column(f) * s_ref[...] + _as_column(1.0 - f) * i_ref[...]
    so_ref[...] = s_new
    o = jnp.sum(_as_column(q_ref[...]) * s_new, axis=0, keepdims=True)
    o_ref[...] = _gated_head_norm(o, gn_ref[...], g_ref[...])


def hgrn_step(proj, lb, gnorm, state):
    bsz = proj.shape[0]
    d = proj.shape[1] // 4
    dv = d // A_HEADS
    nh = A_HEADS
    proj3 = proj.reshape(bsz, 1, 4 * d)

    def col(off):
        return pl.BlockSpec((None, 1, dv), lambda b, h: (b, 0, off * nh + h))

    o, s_new = pl.pallas_call(
        _hgrn_step_kernel,
        grid=(bsz, nh),
        in_specs=[col(0), col(1), col(2), col(3),
                  pl.BlockSpec((1, A_DK), lambda b, h: (0, h)),
                  pl.BlockSpec((1, dv), lambda b, h: (0, 0)),
                  pl.BlockSpec((None, None, A_DK, dv), lambda b, h: (b, h, 0, 0))],
        out_specs=[pl.BlockSpec((None, 1, dv), lambda b, h: (b, 0, h)),
                   pl.BlockSpec((None, None, A_DK, dv), lambda b, h: (b, h, 0, 0))],
        out_shape=[jax.ShapeDtypeStruct((bsz, 1, d), F32),
                   jax.ShapeDtypeStruct(state.shape, F32)],
        compiler_params=_params("parallel", "parallel"),
        name="hgrn_step",
    )(proj3, proj3, proj3, proj3, lb.reshape(1, -1), gnorm.reshape(1, -1), state)
    return o.reshape(bsz, d).astype(BF16), s_new


def _alibi_slope(h):
    return 2.0 ** (-8.0 * (h + 1) / B_HEADS)


def _dilated_prompt_kernel(q_ref, kp_ref, vp_ref, kc_ref, vc_ref, o_ref, lse_ref, *, dilation, steps):
    bq = q_ref.shape[0]
    blk = pl.program_id(2)
    qa = lax.broadcasted_iota(jnp.int32, (bq, 2 * bq), 0)
    kb = lax.broadcasted_iota(jnp.int32, (bq, 2 * bq), 1)
    dist = bq + qa - kb
    valid = (dist >= 0) & (dist <= steps) & ((kb >= bq) | (blk > 0))
    distf = (dist * dilation).astype(F32)
    lane = lax.broadcasted_iota(jnp.int32, (bq, LANES), 1)
    scale = B_HEAD_DIM ** -0.5
    group = B_HEADS // B_KV_HEADS
    lse_tile = jnp.zeros((bq, LANES), F32)
    for kh in range(B_KV_HEADS):
        cs = slice(kh * B_HEAD_DIM, (kh + 1) * B_HEAD_DIM)
        keys = jnp.concatenate([kp_ref[:, cs], kc_ref[:, cs]], axis=0).astype(BF16)
        vals = jnp.concatenate([vp_ref[:, cs], vc_ref[:, cs]], axis=0).astype(BF16)
        for gq in range(group):
            h = kh * group + gq
            hs = slice(h * B_HEAD_DIM, (h + 1) * B_HEAD_DIM)
            s = _dot_nt(q_ref[:, hs].astype(BF16), keys) * scale - _alibi_slope(h) * distf
            s = jnp.where(valid, s, NEG_INF)
            m = jnp.max(s, axis=-1, keepdims=True)
            p = jnp.exp(s - m)
            den = jnp.sum(p, axis=-1, keepdims=True)
            o_ref[:, hs] = _dot(p.astype(BF16), vals) / den
            lse_tile = jnp.where(lane == h, m + jnp.log(den), lse_tile)
    lse_ref[...] = lse_tile


def dilated_prompt(qp, kv, branch, batch, seq):
    dil = B_DILATIONS[branch]
    steps = B_WINDOWS[branch] // dil
    n = seq // dil
    bq = min(steps, n)
    hd = B_HEADS * B_HEAD_DIM
    kw = B_KV_HEADS * B_HEAD_DIM
    q3 = qp.reshape(batch, n, dil * hd)
    kv3 = kv.reshape(batch, n, dil * N_BRANCHES * 2 * kw)
    kcol = 2 * branch

    def kvspec(off, prev):
        if prev:
            return pl.BlockSpec((None, bq, kw),
                                lambda b, c, i: (b, jnp.maximum(i - 1, 0), c * 2 * N_BRANCHES + kcol + off))
        return pl.BlockSpec((None, bq, kw), lambda b, c, i: (b, i, c * 2 * N_BRANCHES + kcol + off))

    o, lse = pl.pallas_call(
        functools.partial(_dilated_prompt_kernel, dilation=dil, steps=steps),
        grid=(batch, dil, n // bq),
        in_specs=[pl.BlockSpec((None, bq, hd), lambda b, c, i: (b, i, c)),
                  kvspec(0, True), kvspec(1, True), kvspec(0, False), kvspec(1, False)],
        out_specs=[pl.BlockSpec((None, bq, hd), lambda b, c, i: (b, i, c)),
                   pl.BlockSpec((None, bq, LANES), lambda b, c, i: (b, i, c))],
        out_shape=[jax.ShapeDtypeStruct((batch, n, dil * hd), F32),
                   jax.ShapeDtypeStruct((batch, n, dil * LANES), F32)],
        compiler_params=_params("parallel", "parallel", "arbitrary"),
        name=f"dilated_prompt_r{dil}",
    )(q3, kv3, kv3, kv3, kv3)
    return o.reshape(batch * seq, hd), lse.reshape(batch * seq, LANES)


def _branch_mix_kernel(o0_ref, o1_ref, o2_ref, l0_ref, l1_ref, l2_ref, out_ref):
    l0, l1, l2 = l0_ref[...], l1_ref[...], l2_ref[...]
    m = jnp.maximum(jnp.maximum(l0, l1), l2)
    e0, e1, e2 = jnp.exp(l0 - m), jnp.exp(l1 - m), jnp.exp(l2 - m)
    tot = e0 + e1 + e2
    w0, w1, w2 = e0 / tot, e1 / tot, e2 / tot
    for h in range(B_HEADS):
        hs = slice(h * B_HEAD_DIM, (h + 1) * B_HEAD_DIM)
        mix = (w0[:, h:h + 1] * o0_ref[:, hs] + w1[:, h:h + 1] * o1_ref[:, hs]
               + w2[:, h:h + 1] * o2_ref[:, hs])
        out_ref[:, hs] = mix.astype(BF16)


def branch_mix(outs, lses):
    m, hd = outs[0].shape
    tm = _row_tile(m, 256)
    ospec = pl.BlockSpec((tm, hd), lambda i: (i, 0))
    lspec = pl.BlockSpec((tm, LANES), lambda i: (i, 0))
    return pl.pallas_call(
        _branch_mix_kernel,
        grid=(m // tm,),
        in_specs=[ospec] * 3 + [lspec] * 3,
        out_specs=ospec,
        out_shape=jax.ShapeDtypeStruct((m, hd), BF16),
        compiler_params=_params("parallel"),
        name="branch_mix",
    )(*outs, *lses)


def _dilated_sample_kernel(q_ref, new_ref, c0_ref, c1_ref, c2_ref, o_ref):
    caches = (c0_ref, c1_ref, c2_ref)
    nrow = c0_ref.shape[0]
    kw = B_KV_HEADS * B_HEAD_DIM
    scale = B_HEAD_DIM ** -0.5
    group = B_HEADS // B_KV_HEADS
    rowi = lax.broadcasted_iota(jnp.int32, (nrow, 1), 0)
    for h in range(B_HEADS):
        kh = h // group
        qh = q_ref[:, h * B_HEAD_DIM:(h + 1) * B_HEAD_DIM]
        outs, lses = [], []
        for z in range(N_BRANCHES):
            kc = caches[z][:, kh * B_HEAD_DIM:(kh + 1) * B_HEAD_DIM]
            vc = caches[z][:, kw + kh * B_HEAD_DIM:kw + (kh + 1) * B_HEAD_DIM]
            base = z * 2 * kw
            kn = new_ref[:, base + kh * B_HEAD_DIM:base + (kh + 1) * B_HEAD_DIM]
            vn = new_ref[:, base + kw + kh * B_HEAD_DIM:base + kw + (kh + 1) * B_HEAD_DIM]
            dist = ((nrow - rowi) * B_DILATIONS[z]).astype(F32)
            sc = jnp.sum(kc * qh, axis=-1, keepdims=True) * scale - _alibi_slope(h) * dist
            sn = jnp.sum(kn * qh, axis=-1, keepdims=True) * scale
            m = jnp.maximum(jnp.max(sc, axis=0, keepdims=True), sn)
            pc = jnp.exp(sc - m)
            pn = jnp.exp(sn - m)
            den = jnp.sum(pc, axis=0, keepdims=True) + pn
            outs.append((jnp.sum(pc * vc, axis=0, keepdims=True) + pn * vn) / den)
            lses.append(m + jnp.log(den))
        mm = jnp.maximum(jnp.maximum(lses[0], lses[1]), lses[2])
        es = [jnp.exp(l - mm) for l in lses]
        tot = es[0] + es[1] + es[2]
        mix = (es[0] / tot) * outs[0] + (es[1] / tot) * outs[1] + (es[2] / tot) * outs[2]
        o_ref[:, h * B_HEAD_DIM:(h + 1) * B_HEAD_DIM] = mix


def dilated_sample(qs, kv_new, caches):
    bsz, hd = qs.shape
    kw2 = 2 * B_KV_HEADS * B_HEAD_DIM
    in_specs = [pl.BlockSpec((None, 1, hd), lambda b: (b, 0, 0)),
                pl.BlockSpec((None, 1, kv_new.shape[1]), lambda b: (b, 0, 0))]
    args = [qs.reshape(bsz, 1, hd), kv_new.reshape(bsz, 1, -1)]
    for z, cache in enumerate(caches):
        dil = B_DILATIONS[z]
        steps = B_WINDOWS[z] // dil
        assert cache.shape[1] == steps * dil
        args.append(cache.reshape(bsz, steps, dil * kw2))
        in_specs.append(pl.BlockSpec((None, steps, kw2), lambda b: (b, 0, 0)))
    out = pl.pallas_call(
        _dilated_sample_kernel,
        grid=(bsz,),
        in_specs=in_specs,
        out_specs=pl.BlockSpec((None, 1, hd), lambda b: (b, 0, 0)),
        out_shape=jax.ShapeDtypeStruct((bsz, 1, hd), F32),
        compiler_params=_params("parallel"),
        name="dilated_sample",
    )(*args)
    return out.reshape(bsz, hd).astype(BF16)


def _topk_rows(x, payload=None):
    r = x.shape[0]
    iota = lax.broadcasted_iota(jnp.int32, x.shape, 0).astype(F32)
    vals, idxs = [], []
    for _ in range(P_TOPK):
        m = jnp.max(x, axis=0, keepdims=True)
        am = jnp.min(jnp.where(x == m, iota, float(r)), axis=0, keepdims=True)
        hit = iota == am
        vals.append(m)
        if payload is None:
            idxs.append(am)
        else:
            idxs.append(jnp.max(jnp.where(hit, payload, 0.0), axis=0, keepdims=True))
        x = jnp.where(hit, NEG_INF, x)
    return jnp.concatenate(vals, axis=0), jnp.concatenate(idxs, axis=0)


def _peer_route_kernel(q_ref, keys_ref, idx_ref, gate_ref):
    idx_rows, gate_rows = [], []
    for h in range(P_HEADS):
        sub = []
        for c in range(2):
            off = (h * 2 + c) * P_HALF
            qhc = q_ref[:, off:off + P_HALF].astype(BF16)
            s_t = _dot_nt(keys_ref[h, c].astype(BF16), qhc)
            sub.append(_topk_rows(s_t))
        (sv0, si0), (sv1, si1) = sub
        cand = jnp.concatenate([sv0[a:a + 1] + sv1 for a in range(P_TOPK)], axis=0)
        cidx = jnp.concatenate([si0[a:a + 1] * P_NKEYS + si1 for a in range(P_TOPK)], axis=0)
        top_s, eidx = _topk_rows(cand, cidx)
        e = jnp.exp(top_s - top_s[0:1])
        gate_rows.append(e / jnp.sum(e, axis=0, keepdims=True))
        idx_rows.append(eidx)
    idx_ref[...] = jnp.concatenate(idx_rows, axis=0).T.astype(jnp.int32)
    gate_ref[...] = jnp.concatenate(gate_rows, axis=0).T


def peer_route(q, keys):
    t, qd = q.shape
    tt = _row_tile(t, LANES)
    nsel = P_HEADS * P_TOPK
    return pl.pallas_call(
        _peer_route_kernel,
        grid=(t // tt,),
        in_specs=[pl.BlockSpec((tt, qd), lambda i: (i, 0)),
                  pl.BlockSpec(keys.shape, lambda i: (0, 0, 0, 0))],
        out_specs=[pl.BlockSpec((tt, nsel), lambda i: (i, 0)),
                   pl.BlockSpec((tt, nsel), lambda i: (i, 0))],
        out_shape=[jax.ShapeDtypeStruct((t, nsel), jnp.int32),
                   jax.ShapeDtypeStruct((t, nsel), F32)],
        compiler_params=_params("parallel"),
        name="peer_route",
    )(q, keys)


def _gelu_exact(x):
    return 0.5 * x * (1.0 + lax.erf(x * (1.0 / math.sqrt(2.0))))


PEER_SLOTS = 4
PEER_AHEAD = PEER_SLOTS - 1
IDX_PAD = SUBLANES


def _peer_expert_kernel(idx_hbm, gate_ref, h_ref, g_ref, fg_ref, tab_hbm, out_ref,
                        idx_smem, *scratch, final_norm):
    bufs = scratch[:PEER_SLOTS]
    xn_ref, gcol_ref, idx_sems, row_sems = scratch[PEER_SLOTS:]
    tt, d = h_ref.shape
    nsel = gate_ref.shape[1]
    nchunk = d // LANES
    per = nsel // (2 * nchunk)
    step = pl.program_id(0)
    nsteps = pl.num_programs(0)
    cur = lax.rem(step, 2)

    def idx_copy(s, slot):
        return pltpu.make_async_copy(idx_hbm.at[pl.ds(s * tt, tt + IDX_PAD)], idx_smem.at[slot], idx_sems.at[slot])

    @pl.when(step == 0)
    def _():
        idx_copy(0, 0).start()

    xn_ref[...] = _rms(h_ref[...], g_ref[...])
    gate_t = gate_ref[...].T
    for t in range(tt):
        gcol_ref[t] = gate_t[:, t:t + 1]
    idx_copy(step, cur).wait()

    @pl.when(step + 1 < nsteps)
    def _():
        idx_copy(step + 1, 1 - cur).start()

    def issue(trow, slot, k0, k1):
        for k in range(k0, k1):
            e = idx_smem[cur, trow, k]
            pltpu.make_async_copy(tab_hbm.at[e], bufs[slot].at[:, k], row_sems.at[slot]).start()

    def wait_rows(slot):
        pltpu.make_async_copy(bufs[slot], bufs[slot], row_sems.at[slot]).wait()

    @pl.when(step == 0)
    def _():
        for p in range(PEER_AHEAD):
            issue(p, p, 0, nsel)

    def token(t, s):
        nxt = (s + PEER_AHEAD) % PEER_SLOTS
        wait_rows(s)
        x = xn_ref[pl.ds(t, 1), :]
        acc = jnp.zeros((nsel, LANES), F32)
        for j in range(nchunk):
            acc = acc + bufs[s][j] * x[:, j * LANES:(j + 1) * LANES]
            issue(t + PEER_AHEAD, nxt, j * per, (j + 1) * per)
        act = jnp.sum(acc, axis=-1, keepdims=True)
        w = jnp.broadcast_to(gcol_ref[t] * _gelu_exact(act), (nsel, LANES))
        pieces = []
        for j in range(nchunk):
            pieces.append(jnp.sum(bufs[s][nchunk + j] * w, axis=0, keepdims=True))
            issue(t + PEER_AHEAD, nxt, (nchunk + j) * per, (nchunk + j + 1) * per)
        out_ref[pl.ds(t, 1), :] = h_ref[pl.ds(t, 1), :] + jnp.concatenate(pieces, axis=1)

    def ring_turn(i, carry):
        for s in range(PEER_SLOTS):
            token(i * PEER_SLOTS + s, s)
        return carry

    lax.fori_loop(0, tt // PEER_SLOTS, ring_turn, 0)

    @pl.when(step == nsteps - 1)
    def _():
        for p in range(PEER_AHEAD):
            wait_rows(p)

    if final_norm:
        out_ref[...] = _rms(out_ref[...], fg_ref[...])


def peer_experts(h, gain, idx, gate, tab, final_gain=None):
    t, d = h.shape
    nsel = idx.shape[1]
    tt = _row_tile(t, 32)
    assert tt % PEER_SLOTS == 0 and PEER_AHEAD <= IDX_PAD
    final_norm = final_gain is not None
    fg = (final_gain if final_norm else gain).reshape(1, d)
    idx = jnp.pad(idx, ((0, IDX_PAD), (0, 0)))
    return pl.pallas_call(
        functools.partial(_peer_expert_kernel, final_norm=final_norm),
        grid=(t // tt,),
        in_specs=[pl.BlockSpec(memory_space=pl.ANY),
                  pl.BlockSpec((tt, nsel), lambda i: (i, 0)),
                  pl.BlockSpec((tt, d), lambda i: (i, 0)),
                  pl.BlockSpec((1, d), lambda i: (0, 0)),
                  pl.BlockSpec((1, d), lambda i: (0, 0)),
                  pl.BlockSpec(memory_space=pl.ANY)],
        out_specs=pl.BlockSpec((tt, d), lambda i: (i, 0)),
        out_shape=jax.ShapeDtypeStruct((t, d), F32),
        scratch_shapes=[pltpu.SMEM((2, tt + IDX_PAD, nsel), jnp.int32)]
        + [pltpu.VMEM((2 * d // LANES, nsel, LANES), F32) for _ in range(PEER_SLOTS)]
        + [pltpu.VMEM((tt, d), F32),
           pltpu.VMEM((tt, nsel, 1), F32),
           pltpu.SemaphoreType.DMA((2,)),
           pltpu.SemaphoreType.DMA((PEER_SLOTS,))],
        compiler_params=_params("arbitrary"),
        name="peer_experts",
    )(idx, gate, h, gain.reshape(1, d), fg, tab)


def peer_layer(h, gain, wq, keys, uv, final_gain=None):
    q = norm_matmul(h, gain, wq)
    idx, gate = peer_route(q, keys)
    return peer_experts(h, gain, idx, gate, uv, final_gain)


def kernel(x_prompt, x_sample, state_hgrn, cache_win_r1, cache_win_r4, cache_win_r16, norm_mix_a, w_in_a, lb_logits, gnorm_a, w_out_a, norm_kv, w_kv, norm_mix_b, w_q_b, w_out_b, norm_ffn, peer_wq, peer_keys, peer_u, peer_v, norm_final):
    batch, seq, d = x_prompt.shape
    dec = x_sample.shape[0]
    n_a = w_in_a.shape[0]
    depth = norm_ffn.shape[0]
    caches = (cache_win_r1, cache_win_r4, cache_win_r16)
    lb_all = jnp.cumsum(jax.nn.softmax(lb_logits.astype(F32), axis=0), axis=0)
    kw2 = 2 * B_KV_HEADS * B_HEAD_DIM

    hp = x_prompt.reshape(batch * seq, d)
    hs = x_sample.reshape(dec, d)
    states_p, states_s = [], []
    kv_p = kv_s = None
    for layer in range(depth):
        if layer < n_a:
            w_in = w_in_a[layer].astype(BF16)
            w_out = w_out_a[layer].astype(BF16)
            proj_p = norm_matmul(hp, norm_mix_a[layer], w_in)
            o_p, st_p = hgrn_prompt(proj_p, lb_all[layer], gnorm_a[layer], batch, seq)
            hp = matmul_residual(o_p, w_out, hp)
            proj_s = norm_matmul(hs, norm_mix_a[layer], w_in)
            o_s, st_s = hgrn_step(proj_s, lb_all[layer], gnorm_a[layer], state_hgrn[:, layer])
            hs = matmul_residual(o_s, w_out, hs)
            states_p.append(st_p)
            states_s.append(st_s)
        else:
            li = layer - n_a
            if layer == n_a:
                wkv = w_kv.astype(BF16)
                kv_p = norm_matmul(hp, norm_kv, wkv)
                kv_s = norm_matmul(hs, norm_kv, wkv)
            wq = w_q_b[li].astype(BF16)
            w_out = w_out_b[li].astype(BF16)
            q_p = norm_matmul(hp, norm_mix_b[li], wq)
            parts = [dilated_prompt(q_p, kv_p, z, batch, seq) for z in range(N_BRANCHES)]
            mix_p = branch_mix([p[0] for p in parts], [p[1] for p in parts])
            hp = matmul_residual(mix_p, w_out, hp)
            q_s = norm_matmul(hs, norm_mix_b[li], wq)
            mix_s = dilated_sample(q_s, kv_s, caches)
            hs = matmul_residual(mix_s, w_out, hs)
        uv = jnp.concatenate([peer_u[layer], peer_v[layer]], axis=1).reshape(-1, 2 * d // LANES, LANES)
        pwq = peer_wq[layer].astype(BF16)
        fin = norm_final if layer == depth - 1 else None
        hp = peer_layer(hp, norm_ffn[layer], pwq, peer_keys[layer], uv, fin)
        hs = peer_layer(hs, norm_ffn[layer], pwq, peer_keys[layer], uv, fin)

    kv_p6 = kv_p.reshape(batch, seq, N_BRANCHES, 2, B_KV_HEADS, B_HEAD_DIM)
    kv_s6 = kv_s.reshape(dec, 1, N_BRANCHES, 2, B_KV_HEADS, B_HEAD_DIM)
    rows_p = [kv_p6[:, seq - min(B_WINDOWS[z], seq):, z] for z in range(N_BRANCHES)]
    rows_s = [kv_s6[:, :, z] for z in range(N_BRANCHES)]
    return (hp.reshape(batch, seq, d), hs.reshape(dec, 1, d),
            jnp.stack(states_p, axis=1), rows_p[0], rows_p[1], rows_p[2],
            jnp.stack(states_s, axis=1), rows_s[0], rows_s[1], rows_s[2])
```

```python
import functools
import math

import jax
import jax.numpy as jnp
from jax import lax
from jax.experimental import pallas as pl
from jax.experimental.pallas import tpu as pltpu

F32 = jnp.float32
BF16 = jnp.bfloat16
EPS = 1e-6

LANES = 128
SUBLANES = 8
VMEM_LIMIT = 56 * 1024 * 1024

A_HEADS = 16
A_DK = 128
A_CHUNK = 32
B_HEADS = 16
B_HEAD_DIM = 128
B_KV_HEADS = 4
B_WINDOWS = (128, 512, 2048)
B_DILATIONS = (1, 4, 16)
N_BRANCHES = 3
P_HEADS = 8
P_NKEYS = 128
P_TOPK = 16
P_HALF = 128
NEG_INF = float("-inf")


def _params(*sem):
    return pltpu.CompilerParams(dimension_semantics=sem, vmem_limit_bytes=VMEM_LIMIT)


def _dot(a, b):
    return jnp.dot(a, b, preferred_element_type=F32)


def _dot_nt(a, b):
    return lax.dot_general(a, b, (((1,), (1,)), ((), ())), preferred_element_type=F32)


def _dot_tn(a, b):
    return lax.dot_general(a, b, (((0,), (0,)), ((), ())), preferred_element_type=F32)


def _rms(x, g):
    ms = jnp.mean(x * x, axis=-1, keepdims=True)
    return x * lax.rsqrt(ms + EPS) * g


def _row_tile(m, want):
    t = min(m, want)
    while m % t:
        t //= 2
    return t


def _norm_matmul_kernel(x_ref, g_ref, w_ref, o_ref, xn_ref):
    @pl.when(pl.program_id(1) == 0)
    def _():
        xn_ref[...] = _rms(x_ref[...], g_ref[...]).astype(BF16)

    o_ref[...] = _dot(xn_ref[...], w_ref[...])


def norm_matmul(x, g, w):
    m, k = x.shape
    n = w.shape[1]
    tm = _row_tile(m, 512)
    tn = _row_tile(n, 512)
    return pl.pallas_call(
        _norm_matmul_kernel,
        grid=(m // tm, n // tn),
        in_specs=[pl.BlockSpec((tm, k), lambda i, j: (i, 0)),
                  pl.BlockSpec((1, k), lambda i, j: (0, 0)),
                  pl.BlockSpec((k, tn), lambda i, j: (0, j))],
        out_specs=pl.BlockSpec((tm, tn), lambda i, j: (i, j)),
        out_shape=jax.ShapeDtypeStruct((m, n), F32),
        scratch_shapes=[pltpu.VMEM((tm, k), BF16)],
        compiler_params=_params("parallel", "arbitrary"),
        name="norm_matmul",
    )(x, g.reshape(1, k), w)


def _matmul_res_kernel(a_ref, w_ref, r_ref, o_ref):
    o_ref[...] = r_ref[...] + _dot(a_ref[...], w_ref[...])


def matmul_residual(a, w, res):
    m, k = a.shape
    n = w.shape[1]
    tm = _row_tile(m, 512)
    tn = _row_tile(n, 512)
    return pl.pallas_call(
        _matmul_res_kernel,
        grid=(m // tm, n // tn),
        in_specs=[pl.BlockSpec((tm, k), lambda i, j: (i, 0)),
                  pl.BlockSpec((k, tn), lambda i, j: (0, j)),
                  pl.BlockSpec((tm, tn), lambda i, j: (i, j))],
        out_specs=pl.BlockSpec((tm, tn), lambda i, j: (i, j)),
        out_shape=jax.ShapeDtypeStruct((m, n), F32),
        compiler_params=_params("parallel", "arbitrary"),
        name="matmul_residual",
    )(a, w, res)


def _split3(x):
    hi = x.astype(BF16)
    r1 = x - hi.astype(F32)
    mid = r1.astype(BF16)
    lo = (r1 - mid.astype(F32)).astype(BF16)
    return hi, mid, lo


def _gated_head_norm(o, gn, g):
    return _rms(o, gn) * jax.nn.sigmoid(g)


def _hgrn_prompt_kernel(q_ref, f_ref, i_ref, g_ref, lb_ref, gn_ref, o_ref, s_ref):
    seq = q_ref.shape[0]
    grp = LANES
    row = lax.broadcasted_iota(jnp.int32, (grp, grp), 0)
    col = lax.broadcasted_iota(jnp.int32, (grp, grp), 1)
    shift = A_CHUNK.bit_length() - 1
    same = jnp.right_shift(row, shift) == jnp.right_shift(col, shift)
    tri = same & (col <= row)
    l_tri = tri.astype(BF16)
    l_all = same.astype(BF16)
    gn = gn_ref[...]
    nhead = s_ref.shape[0]
    s_ref[...] = jnp.zeros(s_ref.shape, F32)

    def group(gi, carry):
        sl = pl.ds(pl.multiple_of(gi * grp, grp), grp)
        heads = range(nhead)
        cols = [slice(hh * A_DK, (hh + 1) * A_DK) for hh in heads]
        st = [s_ref[hh] for hh in heads]
        f = [lb_ref[:, cs] + (1.0 - lb_ref[:, cs]) * jax.nn.sigmoid(f_ref[sl, cs]) for cs in cols]
        k = [1.0 - fh for fh in f]
        parts = [_split3(jnp.log(fh)) for fh in f]
        b = [_dot(l_tri, p[0]) + _dot(l_tri, p[1]) + _dot(l_tri, p[2]) for p in parts]
        bt = [_dot(l_all, p[0]) + _dot(l_all, p[1]) + _dot(l_all, p[2]) for p in parts]
        qe = [(q_ref[sl, cs] * jnp.exp(bh)).astype(BF16) for cs, bh in zip(cols, b)]
        ke = [(kh * jnp.exp(-bh)).astype(BF16) for kh, bh in zip(k, b)]
        kd = [(kh * jnp.exp(bth - bh)).astype(BF16) for kh, bth, bh in zip(k, bt, b)]
        vb = [i_ref[sl, cs].astype(BF16) for cs in cols]
        att = [jnp.where(tri, _dot_nt(qh, kh), 0.0).astype(BF16) for qh, kh in zip(qe, ke)]
        o_intra = [_dot(ah, vh) for ah, vh in zip(att, vb)]
        dec_t = [jnp.exp(bth).T for bth in bt]
        outs = [[] for _ in heads]
        for c in range(grp // A_CHUNK):
            rows = slice(c * A_CHUNK, (c + 1) * A_CHUNK)
            for hh in heads:
                outs[hh].append(o_intra[hh][rows] + _dot(qe[hh][rows], st[hh].astype(BF16)))
            upd = [_dot_tn(kd[hh][rows], vb[hh][rows]) for hh in heads]
            st = [dec_t[hh][:, c * A_CHUNK:c * A_CHUNK + 1] * st[hh] + upd[hh] for hh in heads]
        for hh in heads:
            o = jnp.concatenate(outs[hh], axis=0)
            o_ref[sl, cols[hh]] = _gated_head_norm(o, gn, g_ref[sl, cols[hh]]).astype(BF16)
            s_ref[hh] = st[hh]
        return carry

    lax.fori_loop(0, seq // grp, group, 0)


HGRN_HEADS_PER_STEP = 4


def hgrn_prompt(proj, lb, gnorm, batch, seq):
    d = proj.shape[1] // 4
    dv = d // A_HEADS
    hb = HGRN_HEADS_PER_STEP
    ng = A_HEADS // hb

    def col(off):
        return pl.BlockSpec((seq, hb * dv), lambda b, h: (b, off * ng + h))

    return pl.pallas_call(
        _hgrn_prompt_kernel,
        grid=(batch, ng),
        in_specs=[col(0), col(1), col(2), col(3),
                  pl.BlockSpec((1, hb * A_DK), lambda b, h: (0, h)),
                  pl.BlockSpec((1, dv), lambda b, h: (0, 0))],
        out_specs=[pl.BlockSpec((seq, hb * dv), lambda b, h: (b, h)),
                   pl.BlockSpec((None, hb, A_DK, dv), lambda b, h: (b, h, 0, 0))],
        out_shape=[jax.ShapeDtypeStruct((batch * seq, d), BF16),
                   jax.ShapeDtypeStruct((batch, A_HEADS, A_DK, dv), F32)],
        compiler_params=_params("parallel", "parallel"),
        name="hgrn_prompt",
    )(proj, proj, proj, proj, lb.reshape(1, -1), gnorm.reshape(1, -1))


def _as_column(rowvec):
    n = rowvec.shape[1]
    return jnp.broadcast_to(rowvec, (n, n)).T


def _hgrn_step_kernel(q_ref, f_ref, i_ref, g_ref, lb_ref, gn_ref, s_ref, o_ref, so_ref):
    f = lb_ref[...] + (1.0 - lb_ref[...]) * jax.nn.sigmoid(f_ref[...])
    s_new = _as_column(f) * s_ref[...] + _as_column(1.0 - f) * i_ref[...]
    so_ref[...] = s_new
    o = jnp.sum(_as_column(q_ref[...]) * s_new, axis=0, keepdims=True)
    o_ref[...] = _gated_head_norm(o, gn_ref[...], g_ref[...])


def hgrn_step(proj, lb, gnorm, state):
    bsz = proj.shape[0]
    d = proj.shape[1] // 4
    dv = d // A_HEADS
    nh = A_HEADS
    proj3 = proj.reshape(bsz, 1, 4 * d)

    def col(off):
        return pl.BlockSpec((None, 1, dv), lambda b, h: (b, 0, off * nh + h))

    o, s_new = pl.pallas_call(
        _hgrn_step_kernel,
        grid=(bsz, nh),
        in_specs=[col(0), col(1), col(2), col(3),
                  pl.BlockSpec((1, A_DK), lambda b, h: (0, h)),
                  pl.BlockSpec((1, dv), lambda b, h: (0, 0)),
                  pl.BlockSpec((None, None, A_DK, dv), lambda b, h: (b, h, 0, 0))],
        out_specs=[pl.BlockSpec((None, 1, dv), lambda b, h: (b, 0, h)),
                   pl.BlockSpec((None, None, A_DK, dv), lambda b, h: (b, h, 0, 0))],
        out_shape=[jax.ShapeDtypeStruct((bsz, 1, d), F32),
                   jax.ShapeDtypeStruct(state.shape, F32)],
        compiler_params=_params("parallel", "parallel"),
        name="hgrn_step",
    )(proj3, proj3, proj3, proj3, lb.reshape(1, -1), gnorm.reshape(1, -1), state)
    return o.reshape(bsz, d).astype(BF16), s_new


def _alibi_slope(h):
    return 2.0 ** (-8.0 * (h + 1) / B_HEADS)


def _dilated_prompt_kernel(q_ref, kp_ref, vp_ref, kc_ref, vc_ref, o_ref, lse_ref, *, dilation, steps):
    bq = q_ref.shape[0]
    blk = pl.program_id(2)
    qa = lax.broadcasted_iota(jnp.int32, (bq, 2 * bq), 0)
    kb = lax.broadcasted_iota(jnp.int32, (bq, 2 * bq), 1)
    dist = bq + qa - kb
    valid = (dist >= 0) & (dist <= steps) & ((kb >= bq) | (blk > 0))
    distf = (dist * dilation).astype(F32)
    lane = lax.broadcasted_iota(jnp.int32, (bq, LANES), 1)
    scale = B_HEAD_DIM ** -0.5
    group = B_HEADS // B_KV_HEADS
    lse_tile = jnp.zeros((bq, LANES), F32)
    for kh in range(B_KV_HEADS):
        cs = slice(kh * B_HEAD_DIM, (kh + 1) * B_HEAD_DIM)
        keys = jnp.concatenate([kp_ref[:, cs], kc_ref[:, cs]], axis=0).astype(BF16)
        vals = jnp.concatenate([vp_ref[:, cs], vc_ref[:, cs]], axis=0).astype(BF16)
        for gq in range(group):
            h = kh * group + gq
            hs = slice(h * B_HEAD_DIM, (h + 1) * B_HEAD_DIM)
            s = _dot_nt(q_ref[:, hs].astype(BF16), keys) * scale - _alibi_slope(h) * distf
            s = jnp.where(valid, s, NEG_INF)
            m = jnp.max(s, axis=-1, keepdims=True)
            p = jnp.exp(s - m)
            den = jnp.sum(p, axis=-1, keepdims=True)
            o_ref[:, hs] = _dot(p.astype(BF16), vals) / den
            lse_tile = jnp.where(lane == h, m + jnp.log(den), lse_tile)
    lse_ref[...] = lse_tile


def dilated_prompt(qp, kv, branch, batch, seq):
    dil = B_DILATIONS[branch]
    steps = B_WINDOWS[branch] // dil
    n = seq // dil
    bq = min(steps, n)
    hd = B_HEADS * B_HEAD_DIM
    kw = B_KV_HEADS * B_HEAD_DIM
    q3 = qp.reshape(batch, n, dil * hd)
    kv3 = kv.reshape(batch, n, dil * N_BRANCHES * 2 * kw)
    kcol = 2 * branch

    def kvspec(off, prev):
        if prev:
            return pl.BlockSpec((None, bq, kw),
                                lambda b, c, i: (b, jnp.maximum(i - 1, 0), c * 2 * N_BRANCHES + kcol + off))
        return pl.BlockSpec((None, bq, kw), lambda b, c, i: (b, i, c * 2 * N_BRANCHES + kcol + off))

    o, lse = pl.pallas_call(
        functools.partial(_dilated_prompt_kernel, dilation=dil, steps=steps),
        grid=(batch, dil, n // bq),
        in_specs=[pl.BlockSpec((None, bq, hd), lambda b, c, i: (b, i, c)),
                  kvspec(0, True), kvspec(1, True), kvspec(0, False), kvspec(1, False)],
        out_specs=[pl.BlockSpec((None, bq, hd), lambda b, c, i: (b, i, c)),
                   pl.BlockSpec((None, bq, LANES), lambda b, c, i: (b, i, c))],
        out_shape=[jax.ShapeDtypeStruct((batch, n, dil * hd), F32),
                   jax.ShapeDtypeStruct((batch, n, dil * LANES), F32)],
        compiler_params=_params("parallel", "parallel", "arbitrary"),
        name=f"dilated_prompt_r{dil}",
    )(q3, kv3, kv3, kv3, kv3)
    return o.reshape(batch * seq, hd), lse.reshape(batch * seq, LANES)


def _branch_mix_kernel(o0_ref, o1_ref, o2_ref, l0_ref, l1_ref, l2_ref, out_ref):
    l0, l1, l2 = l0_ref[...], l1_ref[...], l2_ref[...]
    m = jnp.maximum(jnp.maximum(l0, l1), l2)
    e0, e1, e2 = jnp.exp(l0 - m), jnp.exp(l1 - m), jnp.exp(l2 - m)
    tot = e0 + e1 + e2
    w0, w1, w2 = e0 / tot, e1 / tot, e2 / tot
    for h in range(B_HEADS):
        hs = slice(h * B_HEAD_DIM, (h + 1) * B_HEAD_DIM)
        mix = (w0[:, h:h + 1] * o0_ref[:, hs] + w1[:, h:h + 1] * o1_ref[:, hs]
               + w2[:, h:h + 1] * o2_ref[:, hs])
        out_ref[:, hs] = mix.astype(BF16)


def branch_mix(outs, lses):
    m, hd = outs[0].shape
    tm = _row_tile(m, 256)
    ospec = pl.BlockSpec((tm, hd), lambda i: (i, 0))
    lspec = pl.BlockSpec((tm, LANES), lambda i: (i, 0))
    return pl.pallas_call(
        _branch_mix_kernel,
        grid=(m // tm,),
        in_specs=[ospec] * 3 + [lspec] * 3,
        out_specs=ospec,
        out_shape=jax.ShapeDtypeStruct((m, hd), BF16),
        compiler_params=_params("parallel"),
        name="branch_mix",
    )(*outs, *lses)


def _dilated_sample_kernel(q_ref, new_ref, c0_ref, c1_ref, c2_ref, o_ref):
    caches = (c0_ref, c1_ref, c2_ref)
    nrow = c0_ref.shape[0]
    kw = B_KV_HEADS * B_HEAD_DIM
    scale = B_HEAD_DIM ** -0.5
    group = B_HEADS // B_KV_HEADS
    rowi = lax.broadcasted_iota(jnp.int32, (nrow, 1), 0)
    for h in range(B_HEADS):
        kh = h // group
        qh = q_ref[:, h * B_HEAD_DIM:(h + 1) * B_HEAD_DIM]
        outs, lses = [], []
        for z in range(N_BRANCHES):
            kc = caches[z][:, kh * B_HEAD_DIM:(kh + 1) * B_HEAD_DIM]
            vc = caches[z][:, kw + kh * B_HEAD_DIM:kw + (kh + 1) * B_HEAD_DIM]
            base = z * 2 * kw
            kn = new_ref[:, base + kh * B_HEAD_DIM:base + (kh + 1) * B_HEAD_DIM]
            vn = new_ref[:, base + kw + kh * B_HEAD_DIM:base + kw + (kh + 1) * B_HEAD_DIM]
            dist = ((nrow - rowi) * B_DILATIONS[z]).astype(F32)
            sc = jnp.sum(kc * qh, axis=-1, keepdims=True) * scale - _alibi_slope(h) * dist
            sn = jnp.sum(kn * qh, axis=-1, keepdims=True) * scale
            m = jnp.maximum(jnp.max(sc, axis=0, keepdims=True), sn)
            pc = jnp.exp(sc - m)
            pn = jnp.exp(sn - m)
            den = jnp.sum(pc, axis=0, keepdims=True) + pn
            outs.append((jnp.sum(pc * vc, axis=0, keepdims=True) + pn * vn) / den)
            lses.append(m + jnp.log(den))
        mm = jnp.maximum(jnp.maximum(lses[0], lses[1]), lses[2])
        es = [jnp.exp(l - mm) for l in lses]
        tot = es[0] + es[1] + es[2]
        mix = (es[0] / tot) * outs[0] + (es[1] / tot) * outs[1] + (es[2] / tot) * outs[2]
        o_ref[:, h * B_HEAD_DIM:(h + 1) * B_HEAD_DIM] = mix


def dilated_sample(qs, kv_new, caches):
    bsz, hd = qs.shape
    kw2 = 2 * B_KV_HEADS * B_HEAD_DIM
    in_specs = [pl.BlockSpec((None, 1, hd), lambda b: (b, 0, 0)),
                pl.BlockSpec((None, 1, kv_new.shape[1]), lambda b: (b, 0, 0))]
    args = [qs.reshape(bsz, 1, hd), kv_new.reshape(bsz, 1, -1)]
    for z, cache in enumerate(caches):
        dil = B_DILATIONS[z]
        steps = B_WINDOWS[z] // dil
        assert cache.shape[1] == steps * dil
        args.append(cache.reshape(bsz, steps, dil * kw2))
        in_specs.append(pl.BlockSpec((None, steps, kw2), lambda b: (b, 0, 0)))
    out = pl.pallas_call(
        _dilated_sample_kernel,
        grid=(bsz,),
        in_specs=in_specs,
        out_specs=pl.BlockSpec((None, 1, hd), lambda b: (b, 0, 0)),
        out_shape=jax.ShapeDtypeStruct((bsz, 1, hd), F32),
        compiler_params=_params("parallel"),
        name="dilated_sample",
    )(*args)
    return out.reshape(bsz, hd).astype(BF16)


def _topk_rows(x, payload=None):
    r = x.shape[0]
    iota = lax.broadcasted_iota(jnp.int32, x.shape, 0).astype(F32)
    vals, idxs = [], []
    for _ in range(P_TOPK):
        m = jnp.max(x, axis=0, keepdims=True)
        am = jnp.min(jnp.where(x == m, iota, float(r)), axis=0, keepdims=True)
        hit = iota == am
        vals.append(m)
        if payload is None:
            idxs.append(am)
        else:
            idxs.append(jnp.max(jnp.where(hit, payload, 0.0), axis=0, keepdims=True))
        x = jnp.where(hit, NEG_INF, x)
    return jnp.concatenate(vals, axis=0), jnp.concatenate(idxs, axis=0)


def _peer_route_kernel(q_ref, keys_ref, idx_ref, gate_ref):
    idx_rows, gate_rows = [], []
    for h in range(P_HEADS):
        sub = []
        for c in range(2):
            off = (h * 2 + c) * P_HALF
            qhc = q_ref[:, off:off + P_HALF].astype(BF16)
            s_t = _dot_nt(keys_ref[h, c].astype(BF16), qhc)
            sub.append(_topk_rows(s_t))
        (sv0, si0), (sv1, si1) = sub
        cand = jnp.concatenate([sv0[a:a + 1] + sv1 for a in range(P_TOPK)], axis=0)
        cidx = jnp.concatenate([si0[a:a + 1] * P_NKEYS + si1 for a in range(P_TOPK)], axis=0)
        top_s, eidx = _topk_rows(cand, cidx)
        e = jnp.exp(top_s - top_s[0:1])
        gate_rows.append(e / jnp.sum(e, axis=0, keepdims=True))
        idx_rows.append(eidx)
    idx_ref[...] = jnp.concatenate(idx_rows, axis=0).T.astype(jnp.int32)
    gate_ref[...] = jnp.concatenate(gate_rows, axis=0).T


def peer_route(q, keys):
    t, qd = q.shape
    tt = _row_tile(t, LANES)
    nsel = P_HEADS * P_TOPK
    return pl.pallas_call(
        _peer_route_kernel,
        grid=(t // tt,),
        in_specs=[pl.BlockSpec((tt, qd), lambda i: (i, 0)),
                  pl.BlockSpec(keys.shape, lambda i: (0, 0, 0, 0))],
        out_specs=[pl.BlockSpec((tt, nsel), lambda i: (i, 0)),
                   pl.BlockSpec((tt, nsel), lambda i: (i, 0))],
        out_shape=[jax.ShapeDtypeStruct((t, nsel), jnp.int32),
                   jax.ShapeDtypeStruct((t, nsel), F32)],
        compiler_params=_params("parallel"),
        name="peer_route",
    )(q, keys)


def _gelu_exact(x):
    return 0.5 * x * (1.0 + lax.erf(x * (1.0 / math.sqrt(2.0))))


PEER_SLOTS = 8
PEER_AHEAD = PEER_SLOTS - 1
IDX_PAD = SUBLANES
HI_HALF = -65536


def pack_expert_table(u, v):
    e, d = u.shape
    t = jnp.concatenate([u, v], axis=1).astype(BF16).reshape(e, d // LANES, 2, LANES)
    bits = lax.bitcast_convert_type(t, jnp.uint16).astype(jnp.uint32)
    words = bits[:, :, 0, :] | (bits[:, :, 1, :] << 16)
    return lax.bitcast_convert_type(words, jnp.int32)


def _unpack_pair(words):
    lo = lax.bitcast_convert_type(jnp.left_shift(words, 16), F32)
    hi = lax.bitcast_convert_type(jnp.bitwise_and(words, HI_HALF), F32)
    return lo, hi


def _peer_expert_kernel(idx_hbm, gate_ref, h_ref, g_ref, fg_ref, tab_hbm, out_ref,
                        idx_smem, *scratch, final_norm):
    bufs = scratch[:PEER_SLOTS]
    xn_ref, idx_sems, row_sems = scratch[PEER_SLOTS:]
    tt, d = h_ref.shape
    nsel = gate_ref.shape[1]
    nword = d // (2 * LANES)
    per = nsel // (2 * nword)
    step = pl.program_id(0)
    nsteps = pl.num_programs(0)
    cur = lax.rem(step, 2)

    def idx_copy(s, slot):
        return pltpu.make_async_copy(idx_hbm.at[pl.ds(s * tt, tt + IDX_PAD)], idx_smem.at[slot], idx_sems.at[slot])

    @pl.when(step == 0)
    def _():
        idx_copy(0, 0).start()

    xn_ref[...] = _rms(h_ref[...], g_ref[...])
    idx_copy(step, cur).wait()

    @pl.when(step + 1 < nsteps)
    def _():
        idx_copy(step + 1, 1 - cur).start()

    def issue(trow, slot, k0, k1):
        for k in range(k0, k1):
            e = idx_smem[cur, trow, k]
            pltpu.make_async_copy(tab_hbm.at[e], bufs[slot].at[:, k], row_sems.at[slot]).start()

    def wait_rows(slot):
        pltpu.make_async_copy(bufs[slot], bufs[slot], row_sems.at[slot]).wait()

    @pl.when(step == 0)
    def _():
        for p in range(PEER_AHEAD):
            issue(p, p, 0, nsel)

    def token(t, s):
        nxt = (s + PEER_AHEAD) % PEER_SLOTS
        wait_rows(s)
        x = xn_ref[pl.ds(t, 1), :]
        acc = jnp.zeros((nsel, LANES), F32)
        for j in range(nword):
            lo, hi = _unpack_pair(bufs[s][j])
            acc = acc + lo * x[:, 2 * j * LANES:(2 * j + 1) * LANES]
            acc = acc + hi * x[:, (2 * j + 1) * LANES:(2 * j + 2) * LANES]
            issue(t + PEER_AHEAD, nxt, j * per, (j + 1) * per)
        act = jnp.sum(acc, axis=-1, keepdims=True)
        gate_col = jnp.broadcast_to(gate_ref[pl.ds(t, 1), :], (nsel, nsel)).T
        w = gate_col * _gelu_exact(act)
        pieces = []
        for j in range(nword):
            lo, hi = _unpack_pair(bufs[s][nword + j])
            pieces.append(jnp.sum(lo * w, axis=0, keepdims=True))
            pieces.append(jnp.sum(hi * w, axis=0, keepdims=True))
            issue(t + PEER_AHEAD, nxt, (nword + j) * per, (nword + j + 1) * per)
        out_ref[pl.ds(t, 1), :] = h_ref[pl.ds(t, 1), :] + jnp.concatenate(pieces, axis=1)

    def ring_turn(i, carry):
        for s in range(PEER_SLOTS):
            token(i * PEER_SLOTS + s, s)
        return carry

    lax.fori_loop(0, tt // PEER_SLOTS, ring_turn, 0)

    @pl.when(step == nsteps - 1)
    def _():
        for p in range(PEER_AHEAD):
            wait_rows(p)

    if final_norm:
        out_ref[...] = _rms(out_ref[...], fg_ref[...])


def peer_experts(h, gain, idx, gate, tab, final_gain=None):
    t, d = h.shape
    nsel = idx.shape[1]
    tt = _row_tile(t, 32)
    assert tt % PEER_SLOTS == 0 and PEER_AHEAD <= IDX_PAD and nsel == LANES
    final_norm = final_gain is not None
    fg = (final_gain if final_norm else gain).reshape(1, d)
    idx = jnp.pad(idx, ((0, IDX_PAD), (0, 0)))
    return pl.pallas_call(
        functools.partial(_peer_expert_kernel, final_norm=final_norm),
        grid=(t // tt,),
        in_specs=[pl.BlockSpec(memory_space=pl.ANY),
                  pl.BlockSpec((tt, nsel), lambda i: (i, 0)),
                  pl.BlockSpec((tt, d), lambda i: (i, 0)),
                  pl.BlockSpec((1, d), lambda i: (0, 0)),
                  pl.BlockSpec((1, d), lambda i: (0, 0)),
                  pl.BlockSpec(memory_space=pl.ANY)],
        out_specs=pl.BlockSpec((tt, d), lambda i: (i, 0)),
        out_shape=jax.ShapeDtypeStruct((t, d), F32),
        scratch_shapes=[pltpu.SMEM((2, tt + IDX_PAD, nsel), jnp.int32)]
        + [pltpu.VMEM(tab.shape[1:2] + (nsel, LANES), jnp.int32) for _ in range(PEER_SLOTS)]
        + [pltpu.VMEM((tt, d), F32),
           pltpu.SemaphoreType.DMA((2,)),
           pltpu.SemaphoreType.DMA((PEER_SLOTS,))],
        compiler_params=_params("arbitrary"),
        name="peer_experts",
    )(idx, gate, h, gain.reshape(1, d), fg, tab)


def peer_layer(h, gain, wq, keys, uv, final_gain=None):
    q = norm_matmul(h, gain, wq)
    idx, gate = peer_route(q, keys)
    return peer_experts(h, gain, idx, gate, uv, final_gain)


def kernel(x_prompt, x_sample, state_hgrn, cache_win_r1, cache_win_r4, cache_win_r16, norm_mix_a, w_in_a, lb_logits, gnorm_a, w_out_a, norm_kv, w_kv, norm_mix_b, w_q_b, w_out_b, norm_ffn, peer_wq, peer_keys, peer_u, peer_v, norm_final):
    batch, seq, d = x_prompt.shape
    dec = x_sample.shape[0]
    n_a = w_in_a.shape[0]
    depth = norm_ffn.shape[0]
    caches = (cache_win_r1, cache_win_r4, cache_win_r16)
    lb_all = jnp.cumsum(jax.nn.softmax(lb_logits.astype(F32), axis=0), axis=0)
    kw2 = 2 * B_KV_HEADS * B_HEAD_DIM

    hp = x_prompt.reshape(batch * seq, d)
    hs = x_sample.reshape(dec, d)
    states_p, states_s = [], []
    kv_p = kv_s = None
    for layer in range(depth):
        if layer < n_a:
            w_in = w_in_a[layer].astype(BF16)
            w_out = w_out_a[layer].astype(BF16)
            proj_p = norm_matmul(hp, norm_mix_a[layer], w_in)
            o_p, st_p = hgrn_prompt(proj_p, lb_all[layer], gnorm_a[layer], batch, seq)
            hp = matmul_residual(o_p, w_out, hp)
            proj_s = norm_matmul(hs, norm_mix_a[layer], w_in)
            o_s, st_s = hgrn_step(proj_s, lb_all[layer], gnorm_a[layer], state_hgrn[:, layer])
            hs = matmul_residual(o_s, w_out, hs)
            states_p.append(st_p)
            states_s.append(st_s)
        else:
            li = layer - n_a
            if layer == n_a:
                wkv = w_kv.astype(BF16)
                kv_p = norm_matmul(hp, norm_kv, wkv)
                kv_s = norm_matmul(hs, norm_kv, wkv)
            wq = w_q_b[li].astype(BF16)
            w_out = w_out_b[li].astype(BF16)
            q_p = norm_matmul(hp, norm_mix_b[li], wq)
            parts = [dilated_prompt(q_p, kv_p, z, batch, seq) for z in range(N_BRANCHES)]
            mix_p = branch_mix([p[0] for p in parts], [p[1] for p in parts])
            hp = matmul_residual(mix_p, w_out, hp)
            q_s = norm_matmul(hs, norm_mix_b[li], wq)
            mix_s = dilated_sample(q_s, kv_s, caches)
            hs = matmul_residual(mix_s, w_out, hs)
        uv = pack_expert_table(peer_u[layer], peer_v[layer])
        pwq = peer_wq[layer].astype(BF16)
        fin = norm_final if layer == depth - 1 else None
        hp = peer_layer(hp, norm_ffn[layer], pwq, peer_keys[layer], uv, fin)
        hs = peer_layer(hs, norm_ffn[layer], pwq, peer_keys[layer], uv, fin)

    kv_p6 = kv_p.reshape(batch, seq, N_BRANCHES, 2, B_KV_HEADS, B_HEAD_DIM)
    kv_s6 = kv_s.reshape(dec, 1, N_BRANCHES, 2, B_KV_HEADS, B_HEAD_DIM)
    rows_p = [kv_p6[:, seq - min(B_WINDOWS[z], seq):, z] for z in range(N_BRANCHES)]
    rows_s = [kv_s6[:, :, z] for z in range(N_BRANCHES)]
    return (hp.reshape(batch, seq, d), hs.reshape(dec, 1, d),
            jnp.stack(states_p, axis=1), rows_p[0], rows_p[1], rows_p[2],
            jnp.stack(states_s, axis=1), rows_s[0], rows_s[1], rows_s[2])
```

```python
import functools
import math

import jax
import jax.numpy as jnp
from jax import lax
from jax.experimental import pallas as pl
from jax.experimental.pallas import tpu as pltpu

F32 = jnp.float32
BF16 = jnp.bfloat16
EPS = 1e-6

LANES = 128
SUBLANES = 8
VMEM_LIMIT = 56 * 1024 * 1024

A_HEADS = 16
A_DK = 128
A_CHUNK = 32
B_HEADS = 16
B_HEAD_DIM = 128
B_KV_HEADS = 4
B_WINDOWS = (128, 512, 2048)
B_DILATIONS = (1, 4, 16)
N_BRANCHES = 3
P_HEADS = 8
P_NKEYS = 128
P_TOPK = 16
P_HALF = 128
NEG_INF = float("-inf")


def _params(*sem):
    return pltpu.CompilerParams(dimension_semantics=sem, vmem_limit_bytes=VMEM_LIMIT)


def _dot(a, b):
    return jnp.dot(a, b, preferred_element_type=F32)


def _dot_nt(a, b):
    return lax.dot_general(a, b, (((1,), (1,)), ((), ())), preferred_element_type=F32)


def _dot_tn(a, b):
    return lax.dot_general(a, b, (((0,), (0,)), ((), ())), preferred_element_type=F32)


def _rms(x, g):
    ms = jnp.mean(x * x, axis=-1, keepdims=True)
    return x * lax.rsqrt(ms + EPS) * g


def _row_tile(m, want):
    t = min(m, want)
    while m % t:
        t //= 2
    return t


MM_ROWS = 1024
MM_COLS = 1024


def _norm_matmul_kernel(x_ref, g_ref, w_ref, o_ref, xn_ref):
    @pl.when(pl.program_id(1) == 0)
    def _():
        xn_ref[...] = _rms(x_ref[...], g_ref[...]).astype(BF16)

    o_ref[...] = _dot(xn_ref[...], w_ref[...])


def norm_matmul(x, g, w):
    m, k = x.shape
    n = w.shape[1]
    tm = _row_tile(m, MM_ROWS)
    tn = _row_tile(n, MM_COLS)
    return pl.pallas_call(
        _norm_matmul_kernel,
        grid=(m // tm, n // tn),
        in_specs=[pl.BlockSpec((tm, k), lambda i, j: (i, 0)),
                  pl.BlockSpec((1, k), lambda i, j: (0, 0)),
                  pl.BlockSpec((k, tn), lambda i, j: (0, j))],
        out_specs=pl.BlockSpec((tm, tn), lambda i, j: (i, j)),
        out_shape=jax.ShapeDtypeStruct((m, n), F32),
        scratch_shapes=[pltpu.VMEM((tm, k), BF16)],
        compiler_params=_params("parallel", "arbitrary"),
        name="norm_matmul",
    )(x, g.reshape(1, k), w)


def _matmul_res_kernel(a_ref, w_ref, r_ref, o_ref):
    o_ref[...] = r_ref[...] + _dot(a_ref[...], w_ref[...])


def matmul_residual(a, w, res):
    m, k = a.shape
    n = w.shape[1]
    tm = _row_tile(m, MM_ROWS)
    tn = _row_tile(n, MM_COLS)
    return pl.pallas_call(
        _matmul_res_kernel,
        grid=(m // tm, n // tn),
        in_specs=[pl.BlockSpec((tm, k), lambda i, j: (i, 0)),
                  pl.BlockSpec((k, tn), lambda i, j: (0, j)),
                  pl.BlockSpec((tm, tn), lambda i, j: (i, j))],
        out_specs=pl.BlockSpec((tm, tn), lambda i, j: (i, j)),
        out_shape=jax.ShapeDtypeStruct((m, n), F32),
        compiler_params=_params("parallel", "arbitrary"),
        name="matmul_residual",
    )(a, w, res)


def _split3(x):
    hi = x.astype(BF16)
    r1 = x - hi.astype(F32)
    mid = r1.astype(BF16)
    lo = (r1 - mid.astype(F32)).astype(BF16)
    return hi, mid, lo


def _gated_head_norm(o, gn, g):
    return _rms(o, gn) * jax.nn.sigmoid(g)


def _hgrn_prompt_kernel(q_ref, f_ref, i_ref, g_ref, lb_ref, gn_ref, o_ref, s_ref):
    seq = q_ref.shape[0]
    grp = LANES
    row = lax.broadcasted_iota(jnp.int32, (grp, grp), 0)
    col = lax.broadcasted_iota(jnp.int32, (grp, grp), 1)
    shift = A_CHUNK.bit_length() - 1
    same = jnp.right_shift(row, shift) == jnp.right_shift(col, shift)
    tri = same & (col <= row)
    l_tri = tri.astype(BF16)
    l_all = same.astype(BF16)
    gn = gn_ref[...]
    nhead = s_ref.shape[0]
    s_ref[...] = jnp.zeros(s_ref.shape, F32)

    def group(gi, carry):
        sl = pl.ds(pl.multiple_of(gi * grp, grp), grp)
        heads = range(nhead)
        cols = [slice(hh * A_DK, (hh + 1) * A_DK) for hh in heads]
        st = [s_ref[hh] for hh in heads]
        f = [lb_ref[:, cs] + (1.0 - lb_ref[:, cs]) * jax.nn.sigmoid(f_ref[sl, cs]) for cs in cols]
        k = [1.0 - fh for fh in f]
        parts = [_split3(jnp.log(fh)) for fh in f]
        b = [_dot(l_tri, p[0]) + _dot(l_tri, p[1]) + _dot(l_tri, p[2]) for p in parts]
        bt = [_dot(l_all, p[0]) + _dot(l_all, p[1]) + _dot(l_all, p[2]) for p in parts]
        qe = [(q_ref[sl, cs] * jnp.exp(bh)).astype(BF16) for cs, bh in zip(cols, b)]
        ke = [(kh * jnp.exp(-bh)).astype(BF16) for kh, bh in zip(k, b)]
        kd = [(kh * jnp.exp(bth - bh)).astype(BF16) for kh, bth, bh in zip(k, bt, b)]
        vb = [i_ref[sl, cs].astype(BF16) for cs in cols]
        att = [jnp.where(tri, _dot_nt(qh, kh), 0.0).astype(BF16) for qh, kh in zip(qe, ke)]
        o_intra = [_dot(ah, vh) for ah, vh in zip(att, vb)]
        dec_t = [jnp.exp(bth).T for bth in bt]
        outs = [[] for _ in heads]
        for c in range(grp // A_CHUNK):
            rows = slice(c * A_CHUNK, (c + 1) * A_CHUNK)
            for hh in heads:
                outs[hh].append(o_intra[hh][rows] + _dot(qe[hh][rows], st[hh].astype(BF16)))
            upd = [_dot_tn(kd[hh][rows], vb[hh][rows]) for hh in heads]
            st = [dec_t[hh][:, c * A_CHUNK:c * A_CHUNK + 1] * st[hh] + upd[hh] for hh in heads]
        for hh in heads:
            o = jnp.concatenate(outs[hh], axis=0)
            o_ref[sl, cols[hh]] = _gated_head_norm(o, gn, g_ref[sl, cols[hh]]).astype(BF16)
            s_ref[hh] = st[hh]
        return carry

    lax.fori_loop(0, seq // grp, group, 0)


HGRN_HEADS_PER_STEP = 4


def hgrn_prompt(proj, lb, gnorm, batch, seq):
    d = proj.shape[1] // 4
    dv = d // A_HEADS
    hb = HGRN_HEADS_PER_STEP
    ng = A_HEADS // hb

    def col(off):
        return pl.BlockSpec((seq, hb * dv), lambda b, h: (b, off * ng + h))

    return pl.pallas_call(
        _hgrn_prompt_kernel,
        grid=(batch, ng),
        in_specs=[col(0), col(1), col(2), col(3),
                  pl.BlockSpec((1, hb * A_DK), lambda b, h: (0, h)),
                  pl.BlockSpec((1, dv), lambda b, h: (0, 0))],
        out_specs=[pl.BlockSpec((seq, hb * dv), lambda b, h: (b, h)),
                   pl.BlockSpec((None, hb, A_DK, dv), lambda b, h: (b, h, 0, 0))],
        out_shape=[jax.ShapeDtypeStruct((batch * seq, d), BF16),
                   jax.ShapeDtypeStruct((batch, A_HEADS, A_DK, dv), F32)],
        compiler_params=_params("parallel", "parallel"),
        name="hgrn_prompt",
    )(proj, proj, proj, proj, lb.reshape(1, -1), gnorm.reshape(1, -1))


def _as_column(rowvec):
    n = rowvec.shape[1]
    return jnp.broadcast_to(rowvec, (n, n)).T


def _hgrn_step_kernel(p_ref, lb_ref, gn_ref, s_ref, o_ref, so_ref):
    nh = s_ref.shape[0]
    d = nh * A_DK
    for h in range(nh):
        cs = slice(h * A_DK, (h + 1) * A_DK)
        q = p_ref[:, h * A_DK:(h + 1) * A_DK]
        f_raw = p_ref[:, d + h * A_DK:d + (h + 1) * A_DK]
        v = p_ref[:, 2 * d + h * A_DK:2 * d + (h + 1) * A_DK]
        g = p_ref[:, 3 * d + h * A_DK:3 * d + (h + 1) * A_DK]
        f = lb_ref[:, cs] + (1.0 - lb_ref[:, cs]) * jax.nn.sigmoid(f_raw)
        s_new = _as_column(f) * s_ref[h] + _as_column(1.0 - f) * v
        so_ref[h] = s_new
        o = jnp.sum(_as_column(q) * s_new, axis=0, keepdims=True)
        o_ref[:, cs] = _gated_head_norm(o, gn_ref[...], g)


def hgrn_step(proj, lb, gnorm, state):
    bsz = proj.shape[0]
    d = proj.shape[1] // 4
    dv = d // A_HEADS
    assert dv == A_DK
    sspec = pl.BlockSpec((None, A_HEADS, A_DK, dv), lambda b: (b, 0, 0, 0))
    o, s_new = pl.pallas_call(
        _hgrn_step_kernel,
        grid=(bsz,),
        in_specs=[pl.BlockSpec((None, 1, 4 * d), lambda b: (b, 0, 0)),
                  pl.BlockSpec((1, d), lambda b: (0, 0)),
                  pl.BlockSpec((1, dv), lambda b: (0, 0)),
                  sspec],
        out_specs=[pl.BlockSpec((None, 1, d), lambda b: (b, 0, 0)), sspec],
        out_shape=[jax.ShapeDtypeStruct((bsz, 1, d), F32),
                   jax.ShapeDtypeStruct(state.shape, F32)],
        compiler_params=_params("parallel"),
        name="hgrn_step",
    )(proj.reshape(bsz, 1, 4 * d), lb.reshape(1, -1), gnorm.reshape(1, -1), state)
    return o.reshape(bsz, d).astype(BF16), s_new


def _alibi_slope(h):
    return 2.0 ** (-8.0 * (h + 1) / B_HEADS)


def _dilated_prompt_kernel(q_ref, kp_ref, vp_ref, kc_ref, vc_ref, o_ref, lse_ref, *, dilation, steps):
    bq = q_ref.shape[0]
    blk = pl.program_id(2)
    qa = lax.broadcasted_iota(jnp.int32, (bq, 2 * bq), 0)
    kb = lax.broadcasted_iota(jnp.int32, (bq, 2 * bq), 1)
    dist = bq + qa - kb
    valid = (dist >= 0) & (dist <= steps) & ((kb >= bq) | (blk > 0))
    distf = (dist * dilation).astype(F32)
    lane = lax.broadcasted_iota(jnp.int32, (bq, LANES), 1)
    scale = B_HEAD_DIM ** -0.5
    group = B_HEADS // B_KV_HEADS
    lse_tile = jnp.zeros((bq, LANES), F32)
    for kh in range(B_KV_HEADS):
        cs = slice(kh * B_HEAD_DIM, (kh + 1) * B_HEAD_DIM)
        keys = jnp.concatenate([kp_ref[:, cs], kc_ref[:, cs]], axis=0).astype(BF16)
        vals = jnp.concatenate([vp_ref[:, cs], vc_ref[:, cs]], axis=0).astype(BF16)
        for gq in range(group):
            h = kh * group + gq
            hs = slice(h * B_HEAD_DIM, (h + 1) * B_HEAD_DIM)
            s = _dot_nt(q_ref[:, hs].astype(BF16), keys) * scale - _alibi_slope(h) * distf
            s = jnp.where(valid, s, NEG_INF)
            m = jnp.max(s, axis=-1, keepdims=True)
            p = jnp.exp(s - m)
            den = jnp.sum(p, axis=-1, keepdims=True)
            o_ref[:, hs] = _dot(p.astype(BF16), vals) / den
            lse_tile = jnp.where(lane == h, m + jnp.log(den), lse_tile)
    lse_ref[...] = lse_tile


def dilated_prompt(qp, kv, branch, batch, seq):
    dil = B_DILATIONS[branch]
    steps = B_WINDOWS[branch] // dil
    n = seq // dil
    bq = min(steps, n)
    hd = B_HEADS * B_HEAD_DIM
    kw = B_KV_HEADS * B_HEAD_DIM
    q3 = qp.reshape(batch, n, dil * hd)
    kv3 = kv.reshape(batch, n, dil * N_BRANCHES * 2 * kw)
    kcol = 2 * branch

    def kvspec(off, prev):
        if prev:
            return pl.BlockSpec((None, bq, kw),
                                lambda b, c, i: (b, jnp.maximum(i - 1, 0), c * 2 * N_BRANCHES + kcol + off))
        return pl.BlockSpec((None, bq, kw), lambda b, c, i: (b, i, c * 2 * N_BRANCHES + kcol + off))

    o, lse = pl.pallas_call(
        functools.partial(_dilated_prompt_kernel, dilation=dil, steps=steps),
        grid=(batch, dil, n // bq),
        in_specs=[pl.BlockSpec((None, bq, hd), lambda b, c, i: (b, i, c)),
                  kvspec(0, True), kvspec(1, True), kvspec(0, False), kvspec(1, False)],
        out_specs=[pl.BlockSpec((None, bq, hd), lambda b, c, i: (b, i, c)),
                   pl.BlockSpec((None, bq, LANES), lambda b, c, i: (b, i, c))],
        out_shape=[jax.ShapeDtypeStruct((batch, n, dil * hd), F32),
                   jax.ShapeDtypeStruct((batch, n, dil * LANES), F32)],
        compiler_params=_params("parallel", "parallel", "arbitrary"),
        name=f"dilated_prompt_r{dil}",
    )(q3, kv3, kv3, kv3, kv3)
    return o.reshape(batch * seq, hd), lse.reshape(batch * seq, LANES)


def _branch_mix_kernel(o0_ref, o1_ref, o2_ref, l0_ref, l1_ref, l2_ref, out_ref):
    l0, l1, l2 = l0_ref[...], l1_ref[...], l2_ref[...]
    m = jnp.maximum(jnp.maximum(l0, l1), l2)
    e0, e1, e2 = jnp.exp(l0 - m), jnp.exp(l1 - m), jnp.exp(l2 - m)
    tot = e0 + e1 + e2
    w0, w1, w2 = e0 / tot, e1 / tot, e2 / tot
    for h in range(B_HEADS):
        hs = slice(h * B_HEAD_DIM, (h + 1) * B_HEAD_DIM)
        mix = (w0[:, h:h + 1] * o0_ref[:, hs] + w1[:, h:h + 1] * o1_ref[:, hs]
               + w2[:, h:h + 1] * o2_ref[:, hs])
        out_ref[:, hs] = mix.astype(BF16)


def branch_mix(outs, lses):
    m, hd = outs[0].shape
    tm = _row_tile(m, 256)
    ospec = pl.BlockSpec((tm, hd), lambda i: (i, 0))
    lspec = pl.BlockSpec((tm, LANES), lambda i: (i, 0))
    return pl.pallas_call(
        _branch_mix_kernel,
        grid=(m // tm,),
        in_specs=[ospec] * 3 + [lspec] * 3,
        out_specs=ospec,
        out_shape=jax.ShapeDtypeStruct((m, hd), BF16),
        compiler_params=_params("parallel"),
        name="branch_mix",
    )(*outs, *lses)


def _dilated_sample_kernel(q_ref, new_ref, c0_ref, c1_ref, c2_ref, o_ref):
    caches = (c0_ref, c1_ref, c2_ref)
    nrow = c0_ref.shape[0]
    kw = B_KV_HEADS * B_HEAD_DIM
    scale = B_HEAD_DIM ** -0.5
    group = B_HEADS // B_KV_HEADS
    rowi = lax.broadcasted_iota(jnp.int32, (nrow, 1), 0)
    for h in range(B_HEADS):
        kh = h // group
        qh = q_ref[:, h * B_HEAD_DIM:(h + 1) * B_HEAD_DIM]
        outs, lses = [], []
        for z in range(N_BRANCHES):
            kc = caches[z][:, kh, :]
            vc = caches[z][:, B_KV_HEADS + kh, :]
            base = z * 2 * kw
            kn = new_ref[:, base + kh * B_HEAD_DIM:base + (kh + 1) * B_HEAD_DIM]
            vn = new_ref[:, base + kw + kh * B_HEAD_DIM:base + kw + (kh + 1) * B_HEAD_DIM]
            dist = ((nrow - rowi) * B_DILATIONS[z]).astype(F32)
            sc = jnp.sum(kc * qh, axis=-1, keepdims=True) * scale - _alibi_slope(h) * dist
            sn = jnp.sum(kn * qh, axis=-1, keepdims=True) * scale
            m = jnp.maximum(jnp.max(sc, axis=0, keepdims=True), sn)
            pc = jnp.exp(sc - m)
            pn = jnp.exp(sn - m)
            den = jnp.sum(pc, axis=0, keepdims=True) + pn
            outs.append((jnp.sum(pc * vc, axis=0, keepdims=True) + pn * vn) / den)
            lses.append(m + jnp.log(den))
        mm = jnp.maximum(jnp.maximum(lses[0], lses[1]), lses[2])
        es = [jnp.exp(l - mm) for l in lses]
        tot = es[0] + es[1] + es[2]
        mix = (es[0] / tot) * outs[0] + (es[1] / tot) * outs[1] + (es[2] / tot) * outs[2]
        o_ref[:, h * B_HEAD_DIM:(h + 1) * B_HEAD_DIM] = mix


def dilated_sample(qs, kv_new, caches):
    bsz, hd = qs.shape
    in_specs = [pl.BlockSpec((None, 1, hd), lambda b: (b, 0, 0)),
                pl.BlockSpec((None, 1, kv_new.shape[1]), lambda b: (b, 0, 0))]
    args = [qs.reshape(bsz, 1, hd), kv_new.reshape(bsz, 1, -1)]
    for z, cache in enumerate(caches):
        dil = B_DILATIONS[z]
        steps = B_WINDOWS[z] // dil
        assert cache.shape[1] == steps * dil
        args.append(cache.reshape(bsz, steps, dil, 2 * B_KV_HEADS, B_HEAD_DIM))
        in_specs.append(pl.BlockSpec((None, steps, None, 2 * B_KV_HEADS, B_HEAD_DIM), lambda b: (b, 0, 0, 0, 0)))
    out = pl.pallas_call(
        _dilated_sample_kernel,
        grid=(bsz,),
        in_specs=in_specs,
        out_specs=pl.BlockSpec((None, 1, hd), lambda b: (b, 0, 0)),
        out_shape=jax.ShapeDtypeStruct((bsz, 1, hd), F32),
        compiler_params=_params("parallel"),
        name="dilated_sample",
    )(*args)
    return out.reshape(bsz, hd).astype(BF16)


def _topk_rows(x, payload=None):
    r = x.shape[0]
    iota = lax.broadcasted_iota(jnp.int32, x.shape, 0).astype(F32)
    vals, idxs = [], []
    for _ in range(P_TOPK):
        m = jnp.max(x, axis=0, keepdims=True)
        am = jnp.min(jnp.where(x == m, iota, float(r)), axis=0, keepdims=True)
        hit = iota == am
        vals.append(m)
        if payload is None:
            idxs.append(am)
        else:
            idxs.append(jnp.max(jnp.where(hit, payload, 0.0), axis=0, keepdims=True))
        x = jnp.where(hit, NEG_INF, x)
    return jnp.concatenate(vals, axis=0), jnp.concatenate(idxs, axis=0)


def _peer_route_kernel(q_ref, keys_ref, idx_ref, gate_ref):
    idx_rows, gate_rows = [], []
    for h in range(P_HEADS):
        sub = []
        for c in range(2):
            off = (h * 2 + c) * P_HALF
            qhc = q_ref[:, off:off + P_HALF].astype(BF16)
            s_t = _dot_nt(keys_ref[h, c].astype(BF16), qhc)
            sub.append(_topk_rows(s_t))
        (sv0, si0), (sv1, si1) = sub
        cand = jnp.concatenate([sv0[a:a + 1] + sv1 for a in range(P_TOPK)], axis=0)
        cidx = jnp.concatenate([si0[a:a + 1] * P_NKEYS + si1 for a in range(P_TOPK)], axis=0)
        top_s, eidx = _topk_rows(cand, cidx)
        e = jnp.exp(top_s - top_s[0:1])
        gate_rows.append(e / jnp.sum(e, axis=0, keepdims=True))
        idx_rows.append(eidx)
    idx_ref[...] = jnp.concatenate(idx_rows, axis=0).T.astype(jnp.int32)
    gate_ref[...] = jnp.concatenate(gate_rows, axis=0).T


def peer_route(q, keys):
    t, qd = q.shape
    tt = _row_tile(t, LANES)
    nsel = P_HEADS * P_TOPK
    return pl.pallas_call(
        _peer_route_kernel,
        grid=(t // tt,),
        in_specs=[pl.BlockSpec((tt, qd), lambda i: (i, 0)),
                  pl.BlockSpec(keys.shape, lambda i: (0, 0, 0, 0))],
        out_specs=[pl.BlockSpec((tt, nsel), lambda i: (i, 0)),
                   pl.BlockSpec((tt, nsel), lambda i: (i, 0))],
        out_shape=[jax.ShapeDtypeStruct((t, nsel), jnp.int32),
                   jax.ShapeDtypeStruct((t, nsel), F32)],
        compiler_params=_params("parallel"),
        name="peer_route",
    )(q, keys)


def _gelu_exact(x):
    return 0.5 * x * (1.0 + lax.erf(x * (1.0 / math.sqrt(2.0))))


PEER_SLOTS = 8
PEER_AHEAD = PEER_SLOTS - 1
IDX_PAD = SUBLANES
HI_HALF = -65536


def _bf16_bits_high(x):
    bits = lax.bitcast_convert_type(x, jnp.int32)
    return bits + 0x7FFF + jnp.bitwise_and(jnp.right_shift(bits, 16), 1)


def _pack_table_kernel(u_ref, v_ref, o_ref):
    nword = u_ref.shape[1] // (2 * LANES)
    for half, src in enumerate((u_ref, v_ref)):
        for j in range(nword):
            lo = _bf16_bits_high(src[:, 2 * j * LANES:(2 * j + 1) * LANES])
            hi = _bf16_bits_high(src[:, (2 * j + 1) * LANES:(2 * j + 2) * LANES])
            words = jnp.bitwise_or(jnp.bitwise_and(jnp.right_shift(lo, 16), 0xFFFF), jnp.bitwise_and(hi, HI_HALF))
            o_ref[:, half * nword + j, :] = words


def pack_expert_table(u, v):
    e, d = u.shape
    te = _row_tile(e, 256)
    return pl.pallas_call(
        _pack_table_kernel,
        grid=(e // te,),
        in_specs=[pl.BlockSpec((te, d), lambda i: (i, 0))] * 2,
        out_specs=pl.BlockSpec((te, d // LANES, LANES), lambda i: (i, 0, 0)),
        out_shape=jax.ShapeDtypeStruct((e, d // LANES, LANES), jnp.int32),
        compiler_params=_params("parallel"),
        name="pack_expert_table",
    )(u, v)


def _unpack_pair(words):
    lo = lax.bitcast_convert_type(jnp.left_shift(words, 16), F32)
    hi = lax.bitcast_convert_type(jnp.bitwise_and(words, HI_HALF), F32)
    return lo, hi


def _peer_expert_kernel(idx_hbm, gate_ref, h_ref, g_ref, fg_ref, tab_hbm, out_ref,
                        idx_smem, *scratch, final_norm):
    bufs = scratch[:PEER_SLOTS]
    xn_ref, idx_sems, row_sems = scratch[PEER_SLOTS:]
    tt, d = h_ref.shape
    nsel = gate_ref.shape[1]
    nword = d // (2 * LANES)
    per = nsel // (2 * nword)
    step = pl.program_id(0)
    nsteps = pl.num_programs(0)
    cur = lax.rem(step, 2)

    def idx_copy(s, slot):
        return pltpu.make_async_copy(idx_hbm.at[pl.ds(s * tt, tt + IDX_PAD)], idx_smem.at[slot], idx_sems.at[slot])

    @pl.when(step == 0)
    def _():
        idx_copy(0, 0).start()

    xn_ref[...] = _rms(h_ref[...], g_ref[...])
    idx_copy(step, cur).wait()

    @pl.when(step + 1 < nsteps)
    def _():
        idx_copy(step + 1, 1 - cur).start()

    def issue(trow, slot, k0, k1):
        for k in range(k0, k1):
            e = idx_smem[cur, trow, k]
            pltpu.make_async_copy(tab_hbm.at[e], bufs[slot].at[:, k], row_sems.at[slot]).start(priority=k % 2)

    def wait_rows(slot):
        pltpu.make_async_copy(bufs[slot], bufs[slot], row_sems.at[slot]).wait()

    @pl.when(step == 0)
    def _():
        for p in range(PEER_AHEAD):
            issue(p, p, 0, nsel)

    def token(t, s):
        nxt = (s + PEER_AHEAD) % PEER_SLOTS
        wait_rows(s)
        x = xn_ref[pl.ds(t, 1), :]
        acc = jnp.zeros((nsel, LANES), F32)
        for j in range(nword):
            lo, hi = _unpack_pair(bufs[s][j])
            acc = acc + lo * x[:, 2 * j * LANES:(2 * j + 1) * LANES]
            acc = acc + hi * x[:, (2 * j + 1) * LANES:(2 * j + 2) * LANES]
            issue(t + PEER_AHEAD, nxt, j * per, (j + 1) * per)
        act = jnp.sum(acc, axis=-1, keepdims=True)
        gate_col = jnp.broadcast_to(gate_ref[pl.ds(t, 1), :], (nsel, nsel)).T
        w = gate_col * _gelu_exact(act)
        pieces = []
        for j in range(nword):
            lo, hi = _unpack_pair(bufs[s][nword + j])
            pieces.append(jnp.sum(lo * w, axis=0, keepdims=True))
            pieces.append(jnp.sum(hi * w, axis=0, keepdims=True))
            issue(t + PEER_AHEAD, nxt, (nword + j) * per, (nword + j + 1) * per)
        out_ref[pl.ds(t, 1), :] = h_ref[pl.ds(t, 1), :] + jnp.concatenate(pieces, axis=1)

    def ring_turn(i, carry):
        for s in range(PEER_SLOTS):
            token(i * PEER_SLOTS + s, s)
        return carry

    lax.fori_loop(0, tt // PEER_SLOTS, ring_turn, 0)

    @pl.when(step == nsteps - 1)
    def _():
        for p in range(PEER_AHEAD):
            wait_rows(p)

    if final_norm:
        out_ref[...] = _rms(out_ref[...], fg_ref[...])


def peer_experts(h, gain, idx, gate, tab, final_gain=None):
    t, d = h.shape
    nsel = idx.shape[1]
    tt = _row_tile(t, 32)
    assert tt % PEER_SLOTS == 0 and PEER_AHEAD <= IDX_PAD and nsel == LANES
    final_norm = final_gain is not None
    fg = (final_gain if final_norm else gain).reshape(1, d)
    idx = jnp.pad(idx, ((0, IDX_PAD), (0, 0)))
    return pl.pallas_call(
        functools.partial(_peer_expert_kernel, final_norm=final_norm),
        grid=(t // tt,),
        in_specs=[pl.BlockSpec(memory_space=pl.ANY),
                  pl.BlockSpec((tt, nsel), lambda i: (i, 0)),
                  pl.BlockSpec((tt, d), lambda i: (i, 0)),
                  pl.BlockSpec((1, d), lambda i: (0, 0)),
                  pl.BlockSpec((1, d), lambda i: (0, 0)),
                  pl.BlockSpec(memory_space=pl.ANY)],
        out_specs=pl.BlockSpec((tt, d), lambda i: (i, 0)),
        out_shape=jax.ShapeDtypeStruct((t, d), F32),
        scratch_shapes=[pltpu.SMEM((2, tt + IDX_PAD, nsel), jnp.int32)]
        + [pltpu.VMEM(tab.shape[1:2] + (nsel, LANES), jnp.int32) for _ in range(PEER_SLOTS)]
        + [pltpu.VMEM((tt, d), F32),
           pltpu.SemaphoreType.DMA((2,)),
           pltpu.SemaphoreType.DMA((PEER_SLOTS,))],
        compiler_params=_params("arbitrary"),
        name="peer_experts",
    )(idx, gate, h, gain.reshape(1, d), fg, tab)


def peer_layer(h, gain, wq, keys, uv, final_gain=None):
    q = norm_matmul(h, gain, wq)
    idx, gate = peer_route(q, keys)
    return peer_experts(h, gain, idx, gate, uv, final_gain)


def kernel(x_prompt, x_sample, state_hgrn, cache_win_r1, cache_win_r4, cache_win_r16, norm_mix_a, w_in_a, lb_logits, gnorm_a, w_out_a, norm_kv, w_kv, norm_mix_b, w_q_b, w_out_b, norm_ffn, peer_wq, peer_keys, peer_u, peer_v, norm_final):
    batch, seq, d = x_prompt.shape
    dec = x_sample.shape[0]
    n_a = w_in_a.shape[0]
    depth = norm_ffn.shape[0]
    caches = (cache_win_r1, cache_win_r4, cache_win_r16)
    lb_all = jnp.cumsum(jax.nn.softmax(lb_logits.astype(F32), axis=0), axis=0)
    kw2 = 2 * B_KV_HEADS * B_HEAD_DIM

    hp = x_prompt.reshape(batch * seq, d)
    hs = x_sample.reshape(dec, d)
    states_p, states_s = [], []
    kv_p = kv_s = None
    for layer in range(depth):
        if layer < n_a:
            w_in = w_in_a[layer].astype(BF16)
            w_out = w_out_a[layer].astype(BF16)
            proj_p = norm_matmul(hp, norm_mix_a[layer], w_in)
            o_p, st_p = hgrn_prompt(proj_p, lb_all[layer], gnorm_a[layer], batch, seq)
            hp = matmul_residual(o_p, w_out, hp)
            proj_s = norm_matmul(hs, norm_mix_a[layer], w_in)
            o_s, st_s = hgrn_step(proj_s, lb_all[layer], gnorm_a[layer], state_hgrn[:, layer])
            hs = matmul_residual(o_s, w_out, hs)
            states_p.append(st_p)
            states_s.append(st_s)
        else:
            li = layer - n_a
            if layer == n_a:
                wkv = w_kv.astype(BF16)
                kv_p = norm_matmul(hp, norm_kv, wkv)
                kv_s = norm_matmul(hs, norm_kv, wkv)
            wq = w_q_b[li].astype(BF16)
            w_out = w_out_b[li].astype(BF16)
            q_p = norm_matmul(hp, norm_mix_b[li], wq)
            parts = [dilated_prompt(q_p, kv_p, z, batch, seq) for z in range(N_BRANCHES)]
            mix_p = branch_mix([p[0] for p in parts], [p[1] for p in parts])
            hp = matmul_residual(mix_p, w_out, hp)
            q_s = norm_matmul(hs, norm_mix_b[li], wq)
            mix_s = dilated_sample(q_s, kv_s, caches)
            hs = matmul_residual(mix_s, w_out, hs)
        uv = pack_expert_table(peer_u[layer], peer_v[layer])
        pwq = peer_wq[layer].astype(BF16)
        fin = norm_final if layer == depth - 1 else None
        hp = peer_layer(hp, norm_ffn[layer], pwq, peer_keys[layer], uv, fin)
        hs = peer_layer(hs, norm_ffn[layer], pwq, peer_keys[layer], uv, fin)

    kv_p6 = kv_p.reshape(batch, seq, N_BRANCHES, 2, B_KV_HEADS, B_HEAD_DIM)
    kv_s6 = kv_s.reshape(dec, 1, N_BRANCHES, 2, B_KV_HEADS, B_HEAD_DIM)
    rows_p = [kv_p6[:, seq - min(B_WINDOWS[z], seq):, z] for z in range(N_BRANCHES)]
    rows_s = [kv_s6[:, :, z] for z in range(N_BRANCHES)]
    return (hp.reshape(batch, seq, d), hs.reshape(dec, 1, d),
            jnp.stack(states_p, axis=1), rows_p[0], rows_p[1], rows_p[2],
            jnp.stack(states_s, axis=1), rows_s[0], rows_s[1], rows_s[2])
```

```python
import functools
import math

import jax
import jax.numpy as jnp
import numpy as np
from jax import lax
from jax.experimental import pallas as pl
from jax.experimental.pallas import tpu as pltpu

F32 = jnp.float32
BF16 = jnp.bfloat16
EPS = 1e-6

LANES = 128
SUBLANES = 8
VMEM_LIMIT = 56 * 1024 * 1024

A_HEADS = 16
A_DK = 128
A_CHUNK = 32
B_HEADS = 16
B_HEAD_DIM = 128
B_KV_HEADS = 4
B_WINDOWS = (128, 512, 2048)
B_DILATIONS = (1, 4, 16)
N_BRANCHES = 3
P_HEADS = 8
P_NKEYS = 128
P_TOPK = 16
P_HALF = 128
NEG_INF = float("-inf")


def _params(*sem):
    return pltpu.CompilerParams(dimension_semantics=sem, vmem_limit_bytes=VMEM_LIMIT)


def _dot(a, b):
    return jnp.dot(a, b, preferred_element_type=F32)


def _dot_nt(a, b):
    return lax.dot_general(a, b, (((1,), (1,)), ((), ())), preferred_element_type=F32)


def _dot_tn(a, b):
    return lax.dot_general(a, b, (((0,), (0,)), ((), ())), preferred_element_type=F32)


def _rms(x, g):
    ms = jnp.mean(x * x, axis=-1, keepdims=True)
    return x * lax.rsqrt(ms + EPS) * g


def _row_tile(m, want):
    t = min(m, want)
    while m % t:
        t //= 2
    return t


MM_ROWS = 1024
MM_COLS = 1024


def _norm_matmul_kernel(x_ref, g_ref, w_ref, o_ref, xn_ref):
    @pl.when(pl.program_id(1) == 0)
    def _():
        xn_ref[...] = _rms(x_ref[...], g_ref[...]).astype(BF16)

    o_ref[...] = _dot(xn_ref[...], w_ref[...])


def norm_matmul(x, g, w):
    m, k = x.shape
    n = w.shape[1]
    tm = _row_tile(m, MM_ROWS)
    tn = _row_tile(n, MM_COLS)
    return pl.pallas_call(
        _norm_matmul_kernel,
        grid=(m // tm, n // tn),
        in_specs=[pl.BlockSpec((tm, k), lambda i, j: (i, 0)),
                  pl.BlockSpec((1, k), lambda i, j: (0, 0)),
                  pl.BlockSpec((k, tn), lambda i, j: (0, j))],
        out_specs=pl.BlockSpec((tm, tn), lambda i, j: (i, j)),
        out_shape=jax.ShapeDtypeStruct((m, n), F32),
        scratch_shapes=[pltpu.VMEM((tm, k), BF16)],
        compiler_params=_params("parallel", "arbitrary"),
        name="norm_matmul",
    )(x, g.reshape(1, k), w)


def _matmul_res_kernel(a_ref, w_ref, r_ref, o_ref):
    o_ref[...] = r_ref[...] + _dot(a_ref[...], w_ref[...])


def matmul_residual(a, w, res):
    m, k = a.shape
    n = w.shape[1]
    tm = _row_tile(m, MM_ROWS)
    tn = _row_tile(n, MM_COLS)
    return pl.pallas_call(
        _matmul_res_kernel,
        grid=(m // tm, n // tn),
        in_specs=[pl.BlockSpec((tm, k), lambda i, j: (i, 0)),
                  pl.BlockSpec((k, tn), lambda i, j: (0, j)),
                  pl.BlockSpec((tm, tn), lambda i, j: (i, j))],
        out_specs=pl.BlockSpec((tm, tn), lambda i, j: (i, j)),
        out_shape=jax.ShapeDtypeStruct((m, n), F32),
        compiler_params=_params("parallel", "arbitrary"),
        name="matmul_residual",
    )(a, w, res)


def _split3(x):
    hi = x.astype(BF16)
    r1 = x - hi.astype(F32)
    mid = r1.astype(BF16)
    lo = (r1 - mid.astype(F32)).astype(BF16)
    return hi, mid, lo


def _gated_head_norm(o, gn, g):
    return _rms(o, gn) * jax.nn.sigmoid(g)


def _hgrn_prompt_kernel(q_ref, f_ref, i_ref, g_ref, lb_ref, gn_ref, o_ref, s_ref):
    seq = q_ref.shape[0]
    grp = LANES
    row = lax.broadcasted_iota(jnp.int32, (grp, grp), 0)
    col = lax.broadcasted_iota(jnp.int32, (grp, grp), 1)
    shift = A_CHUNK.bit_length() - 1
    same = jnp.right_shift(row, shift) == jnp.right_shift(col, shift)
    tri = same & (col <= row)
    l_tri = tri.astype(BF16)
    l_all = same.astype(BF16)
    gn = gn_ref[...]
    nhead = s_ref.shape[0]
    s_ref[...] = jnp.zeros(s_ref.shape, F32)

    def group(gi, carry):
        sl = pl.ds(pl.multiple_of(gi * grp, grp), grp)
        heads = range(nhead)
        cols = [slice(hh * A_DK, (hh + 1) * A_DK) for hh in heads]
        st = [s_ref[hh] for hh in heads]
        f = [lb_ref[:, cs] + (1.0 - lb_ref[:, cs]) * jax.nn.sigmoid(f_ref[sl, cs]) for cs in cols]
        k = [1.0 - fh for fh in f]
        parts = [_split3(jnp.log(fh)) for fh in f]
        b = [_dot(l_tri, p[0]) + _dot(l_tri, p[1]) + _dot(l_tri, p[2]) for p in parts]
        bt = [_dot(l_all, p[0]) + _dot(l_all, p[1]) + _dot(l_all, p[2]) for p in parts]
        qe = [(q_ref[sl, cs] * jnp.exp(bh)).astype(BF16) for cs, bh in zip(cols, b)]
        ke = [(kh * jnp.exp(-bh)).astype(BF16) for kh, bh in zip(k, b)]
        kd = [(kh * jnp.exp(bth - bh)).astype(BF16) for kh, bth, bh in zip(k, bt, b)]
        vb = [i_ref[sl, cs].astype(BF16) for cs in cols]
        att = [jnp.where(tri, _dot_nt(qh, kh), 0.0).astype(BF16) for qh, kh in zip(qe, ke)]
        o_intra = [_dot(ah, vh) for ah, vh in zip(att, vb)]
        dec_t = [jnp.exp(bth).T for bth in bt]
        outs = [[] for _ in heads]
        for c in range(grp // A_CHUNK):
            rows = slice(c * A_CHUNK, (c + 1) * A_CHUNK)
            for hh in heads:
                outs[hh].append(o_intra[hh][rows] + _dot(qe[hh][rows], st[hh].astype(BF16)))
            upd = [_dot_tn(kd[hh][rows], vb[hh][rows]) for hh in heads]
            st = [dec_t[hh][:, c * A_CHUNK:c * A_CHUNK + 1] * st[hh] + upd[hh] for hh in heads]
        for hh in heads:
            o = jnp.concatenate(outs[hh], axis=0)
            o_ref[sl, cols[hh]] = _gated_head_norm(o, gn, g_ref[sl, cols[hh]]).astype(BF16)
            s_ref[hh] = st[hh]
        return carry

    lax.fori_loop(0, seq // grp, group, 0)


HGRN_HEADS_PER_STEP = 4


def hgrn_prompt(proj, lb, gnorm, batch, seq):
    d = proj.shape[1] // 4
    dv = d // A_HEADS
    hb = HGRN_HEADS_PER_STEP
    ng = A_HEADS // hb

    def col(off):
        return pl.BlockSpec((seq, hb * dv), lambda b, h: (b, off * ng + h))

    return pl.pallas_call(
        _hgrn_prompt_kernel,
        grid=(batch, ng),
        in_specs=[col(0), col(1), col(2), col(3),
                  pl.BlockSpec((1, hb * A_DK), lambda b, h: (0, h)),
                  pl.BlockSpec((1, dv), lambda b, h: (0, 0))],
        out_specs=[pl.BlockSpec((seq, hb * dv), lambda b, h: (b, h)),
                   pl.BlockSpec((None, hb, A_DK, dv), lambda b, h: (b, h, 0, 0))],
        out_shape=[jax.ShapeDtypeStruct((batch * seq, d), BF16),
                   jax.ShapeDtypeStruct((batch, A_HEADS, A_DK, dv), F32)],
        compiler_params=_params("parallel", "parallel"),
        name="hgrn_prompt",
    )(proj, proj, proj, proj, lb.reshape(1, -1), gnorm.reshape(1, -1))


def _as_column(rowvec):
    n = rowvec.shape[1]
    return jnp.broadcast_to(rowvec, (n, n)).T


def _hgrn_step_kernel(p_ref, lb_ref, gn_ref, s_ref, o_ref, so_ref):
    nh = s_ref.shape[0]
    d = nh * A_DK
    for h in range(nh):
        cs = slice(h * A_DK, (h + 1) * A_DK)
        q = p_ref[:, h * A_DK:(h + 1) * A_DK]
        f_raw = p_ref[:, d + h * A_DK:d + (h + 1) * A_DK]
        v = p_ref[:, 2 * d + h * A_DK:2 * d + (h + 1) * A_DK]
        g = p_ref[:, 3 * d + h * A_DK:3 * d + (h + 1) * A_DK]
        f = lb_ref[:, cs] + (1.0 - lb_ref[:, cs]) * jax.nn.sigmoid(f_raw)
        s_new = _as_column(f) * s_ref[h] + _as_column(1.0 - f) * v
        so_ref[h] = s_new
        o = jnp.sum(_as_column(q) * s_new, axis=0, keepdims=True)
        o_ref[:, cs] = _gated_head_norm(o, gn_ref[...], g)


def hgrn_step(proj, lb, gnorm, state):
    bsz = proj.shape[0]
    d = proj.shape[1] // 4
    dv = d // A_HEADS
    assert dv == A_DK
    sspec = pl.BlockSpec((None, A_HEADS, A_DK, dv), lambda b: (b, 0, 0, 0))
    o, s_new = pl.pallas_call(
        _hgrn_step_kernel,
        grid=(bsz,),
        in_specs=[pl.BlockSpec((None, 1, 4 * d), lambda b: (b, 0, 0)),
                  pl.BlockSpec((1, d), lambda b: (0, 0)),
                  pl.BlockSpec((1, dv), lambda b: (0, 0)),
                  sspec],
        out_specs=[pl.BlockSpec((None, 1, d), lambda b: (b, 0, 0)), sspec],
        out_shape=[jax.ShapeDtypeStruct((bsz, 1, d), F32),
                   jax.ShapeDtypeStruct(state.shape, F32)],
        compiler_params=_params("parallel"),
        name="hgrn_step",
    )(proj.reshape(bsz, 1, 4 * d), lb.reshape(1, -1), gnorm.reshape(1, -1), state)
    return o.reshape(bsz, d).astype(BF16), s_new


def _alibi_slope(h):
    return 2.0 ** (-8.0 * (h + 1) / B_HEADS)


def _window_block(q_rows, keys, vals, slopes, bias_dist, valid):
    scale = B_HEAD_DIM ** -0.5
    s = [_dot_nt(qg.astype(BF16), keys) for qg in q_rows]
    s = [jnp.where(valid, sg * scale - slopes[:, g:g + 1] * bias_dist, NEG_INF) for g, sg in enumerate(s)]
    m = [jnp.max(sg, axis=-1, keepdims=True) for sg in s]
    p = [jnp.exp(sg - mg) for sg, mg in zip(s, m)]
    den = [jnp.sum(pg, axis=-1, keepdims=True) for pg in p]
    o = [_dot(pg.astype(BF16), vals) / dg for pg, dg in zip(p, den)]
    return [(og, mg + jnp.log(dg)) for og, mg, dg in zip(o, m, den)]


def _dilated_prompt_kernel(slope_ref, *refs):
    group = B_HEADS // B_KV_HEADS
    q_refs = refs[:group]
    kv_refs = refs[group:group + 2 * N_BRANCHES]
    out_ref, o_scr, lse_scr = refs[group + 2 * N_BRANCHES:]
    seq = out_ref.shape[0]
    slopes = slope_ref[...]

    for z in range(N_BRANCHES):
        dil = B_DILATIONS[z]
        steps = B_WINDOWS[z] // dil
        bq = min(steps, seq // dil)
        nblk = seq // (dil * bq)
        k_ref, v_ref = kv_refs[2 * z], kv_refs[2 * z + 1]
        lane = lax.broadcasted_iota(jnp.int32, (bq, LANES), 1)
        qa = lax.broadcasted_iota(jnp.int32, (bq, 2 * bq), 0)
        kb = lax.broadcasted_iota(jnp.int32, (bq, 2 * bq), 1)
        dist2 = bq + qa - kb
        near2 = (dist2 >= 0) & (dist2 <= steps)
        dist1 = dist2[:, bq:]
        near1 = near2[:, bq:]

        def store(rows, res):
            lse_tile = jnp.zeros((bq, LANES), F32)
            for g, (o, lse) in enumerate(res):
                o_scr[z, g, rows, :] = o
                lse_tile = jnp.where(lane == g, lse, lse_tile)
            lse_scr[z, rows, :] = lse_tile

        for c in range(dil):
            def rows_at(i):
                return pl.ds(c + i * (dil * bq), bq, stride=dil)

            first = rows_at(0)
            store(first, _window_block([q[first, :] for q in q_refs], k_ref[first, :].astype(BF16),
                                       v_ref[first, :].astype(BF16), slopes, (dist1 * dil).astype(F32), near1))

            def block(i, carry):
                cur, prev = rows_at(i), rows_at(i - 1)
                keys = jnp.concatenate([k_ref[prev, :], k_ref[cur, :]], axis=0).astype(BF16)
                vals = jnp.concatenate([v_ref[prev, :], v_ref[cur, :]], axis=0).astype(BF16)
                store(cur, _window_block([q[cur, :] for q in q_refs], keys, vals, slopes,
                                         (dist2 * dil).astype(F32), near2))
                return carry

            if nblk > 1:
                lax.fori_loop(1, nblk, block, 0)

    def mix(i, carry):
        rows = pl.ds(pl.multiple_of(i * LANES, LANES), LANES)
        l0, l1, l2 = lse_scr[0, rows, :], lse_scr[1, rows, :], lse_scr[2, rows, :]
        m = jnp.maximum(jnp.maximum(l0, l1), l2)
        e0, e1, e2 = jnp.exp(l0 - m), jnp.exp(l1 - m), jnp.exp(l2 - m)
        tot = e0 + e1 + e2
        w0, w1, w2 = e0 / tot, e1 / tot, e2 / tot
        for g in range(group):
            hs = slice(g * B_HEAD_DIM, (g + 1) * B_HEAD_DIM)
            out_ref[rows, hs] = (w0[:, g:g + 1] * o_scr[0, g, rows, :] + w1[:, g:g + 1] * o_scr[1, g, rows, :]
                                 + w2[:, g:g + 1] * o_scr[2, g, rows, :]).astype(BF16)
        return carry

    lax.fori_loop(0, seq // LANES, mix, 0)


def dilated_prompt(qp, kv, batch, seq):
    group = B_HEADS // B_KV_HEADS
    gw = group * B_HEAD_DIM
    assert seq % LANES == 0 and all(seq % min(B_WINDOWS[z], seq) == 0 for z in range(N_BRANCHES))
    slopes = jnp.exp2(-8.0 * jnp.arange(1, B_HEADS + 1, dtype=F32) / B_HEADS).reshape(B_KV_HEADS, group)
    slopes = jnp.pad(slopes, ((0, 0), (0, LANES - group))).reshape(B_KV_HEADS, 1, LANES)
    q_specs = [pl.BlockSpec((seq, B_HEAD_DIM), lambda b, kh, g=g: (b, kh * group + g)) for g in range(group)]
    kv_specs = []
    for z in range(N_BRANCHES):
        for part in range(2):
            kv_specs.append(pl.BlockSpec((seq, B_HEAD_DIM),
                                         lambda b, kh, z=z, part=part: (b, (2 * z + part) * B_KV_HEADS + kh)))
    return pl.pallas_call(
        _dilated_prompt_kernel,
        grid=(batch, B_KV_HEADS),
        in_specs=[pl.BlockSpec((None, 1, LANES), lambda b, kh: (kh, 0, 0))] + q_specs + kv_specs,
        out_specs=pl.BlockSpec((seq, gw), lambda b, kh: (b, kh)),
        out_shape=jax.ShapeDtypeStruct((batch * seq, B_HEADS * B_HEAD_DIM), BF16),
        scratch_shapes=[pltpu.VMEM((N_BRANCHES, group, seq, B_HEAD_DIM), F32),
                        pltpu.VMEM((N_BRANCHES, seq, LANES), F32)],
        compiler_params=_params("parallel", "parallel"),
        name="dilated_prompt",
    )(slopes, *([qp] * group), *([kv] * (2 * N_BRANCHES)))


def _dilated_sample_kernel(q_ref, new_ref, c0_ref, c1_ref, c2_ref, o_ref):
    caches = (c0_ref, c1_ref, c2_ref)
    nrow = c0_ref.shape[0]
    kw = B_KV_HEADS * B_HEAD_DIM
    scale = B_HEAD_DIM ** -0.5
    group = B_HEADS // B_KV_HEADS
    rowi = lax.broadcasted_iota(jnp.int32, (nrow, 1), 0)
    for h in range(B_HEADS):
        kh = h // group
        qh = q_ref[:, h * B_HEAD_DIM:(h + 1) * B_HEAD_DIM]
        outs, lses = [], []
        for z in range(N_BRANCHES):
            kc = caches[z][:, kh, :]
            vc = caches[z][:, B_KV_HEADS + kh, :]
            base = z * 2 * kw
            kn = new_ref[:, base + kh * B_HEAD_DIM:base + (kh + 1) * B_HEAD_DIM]
            vn = new_ref[:, base + kw + kh * B_HEAD_DIM:base + kw + (kh + 1) * B_HEAD_DIM]
            dist = ((nrow - rowi) * B_DILATIONS[z]).astype(F32)
            sc = jnp.sum(kc * qh, axis=-1, keepdims=True) * scale - _alibi_slope(h) * dist
            sn = jnp.sum(kn * qh, axis=-1, keepdims=True) * scale
            m = jnp.maximum(jnp.max(sc, axis=0, keepdims=True), sn)
            pc = jnp.exp(sc - m)
            pn = jnp.exp(sn - m)
            den = jnp.sum(pc, axis=0, keepdims=True) + pn
            outs.append((jnp.sum(pc * vc, axis=0, keepdims=True) + pn * vn) / den)
            lses.append(m + jnp.log(den))
        mm = jnp.maximum(jnp.maximum(lses[0], lses[1]), lses[2])
        es = [jnp.exp(l - mm) for l in lses]
        tot = es[0] + es[1] + es[2]
        mix = (es[0] / tot) * outs[0] + (es[1] / tot) * outs[1] + (es[2] / tot) * outs[2]
        o_ref[:, h * B_HEAD_DIM:(h + 1) * B_HEAD_DIM] = mix


def dilated_sample(qs, kv_new, caches):
    bsz, hd = qs.shape
    in_specs = [pl.BlockSpec((None, 1, hd), lambda b: (b, 0, 0)),
                pl.BlockSpec((None, 1, kv_new.shape[1]), lambda b: (b, 0, 0))]
    args = [qs.reshape(bsz, 1, hd), kv_new.reshape(bsz, 1, -1)]
    for z, cache in enumerate(caches):
        dil = B_DILATIONS[z]
        steps = B_WINDOWS[z] // dil
        assert cache.shape[1] == steps * dil
        args.append(cache.reshape(bsz, steps, dil, 2 * B_KV_HEADS, B_HEAD_DIM))
        in_specs.append(pl.BlockSpec((None, steps, None, 2 * B_KV_HEADS, B_HEAD_DIM), lambda b: (b, 0, 0, 0, 0)))
    out = pl.pallas_call(
        _dilated_sample_kernel,
        grid=(bsz,),
        in_specs=in_specs,
        out_specs=pl.BlockSpec((None, 1, hd), lambda b: (b, 0, 0)),
        out_shape=jax.ShapeDtypeStruct((bsz, 1, hd), F32),
        compiler_params=_params("parallel"),
        name="dilated_sample",
    )(*args)
    return out.reshape(bsz, hd).astype(BF16)


def _topk_rows(x, payload=None):
    r = x.shape[0]
    iota = lax.broadcasted_iota(jnp.int32, x.shape, 0).astype(F32)
    vals, idxs = [], []
    for _ in range(P_TOPK):
        m = jnp.max(x, axis=0, keepdims=True)
        am = jnp.min(jnp.where(x == m, iota, float(r)), axis=0, keepdims=True)
        hit = iota == am
        vals.append(m)
        if payload is None:
            idxs.append(am)
        else:
            idxs.append(jnp.max(jnp.where(hit, payload, 0.0), axis=0, keepdims=True))
        x = jnp.where(hit, NEG_INF, x)
    return jnp.concatenate(vals, axis=0), jnp.concatenate(idxs, axis=0)


def _staircase_pairs():
    return [(a, b) for a in range(P_TOPK) for b in range(P_TOPK) if (a + 1) * (b + 1) <= P_TOPK]


def _pick_rows(sel, x):
    hi, mid, lo = _split3(x)
    return _dot(sel, hi) + _dot(sel, mid) + _dot(sel, lo)


def _peer_route_kernel(q_ref, keys_ref, sel_ref, idx_ref, gate_ref, *, npair):
    sel_a, sel_b = sel_ref[0], sel_ref[1]
    nrow = sel_a.shape[0]
    pad = jnp.where(lax.broadcasted_iota(jnp.int32, (nrow, 1), 0) < npair, 0.0, NEG_INF)
    idx_rows, gate_rows = [], []
    for h in range(P_HEADS):
        sub = []
        for c in range(2):
            off = (h * 2 + c) * P_HALF
            qhc = q_ref[:, off:off + P_HALF].astype(BF16)
            s_t = _dot_nt(keys_ref[h, c].astype(BF16), qhc)
            sub.append(_topk_rows(s_t))
        (sv0, si0), (sv1, si1) = sub
        cand = _pick_rows(sel_a, sv0) + _pick_rows(sel_b, sv1) + pad
        cidx = _dot(sel_a, si0.astype(BF16)) * P_NKEYS + _dot(sel_b, si1.astype(BF16))
        top_s, eidx = _topk_rows(cand, cidx)
        e = jnp.exp(top_s - top_s[0:1])
        gate_rows.append(e / jnp.sum(e, axis=0, keepdims=True))
        idx_rows.append(eidx)
    idx_ref[...] = jnp.concatenate(idx_rows, axis=0).T.astype(jnp.int32)
    gate_ref[...] = jnp.concatenate(gate_rows, axis=0).T


def peer_route(q, keys):
    t, qd = q.shape
    tt = _row_tile(t, LANES)
    nsel = P_HEADS * P_TOPK
    pairs = _staircase_pairs()
    nrow = -(-len(pairs) // SUBLANES) * SUBLANES
    sel = np.zeros((2, nrow, P_TOPK), np.float32)
    for r, (a, b) in enumerate(pairs):
        sel[0, r, a] = 1.0
        sel[1, r, b] = 1.0
    return pl.pallas_call(
        functools.partial(_peer_route_kernel, npair=len(pairs)),
        grid=(t // tt,),
        in_specs=[pl.BlockSpec((tt, qd), lambda i: (i, 0)),
                  pl.BlockSpec(keys.shape, lambda i: (0, 0, 0, 0)),
                  pl.BlockSpec(sel.shape, lambda i: (0, 0, 0))],
        out_specs=[pl.BlockSpec((tt, nsel), lambda i: (i, 0)),
                   pl.BlockSpec((tt, nsel), lambda i: (i, 0))],
        out_shape=[jax.ShapeDtypeStruct((t, nsel), jnp.int32),
                   jax.ShapeDtypeStruct((t, nsel), F32)],
        compiler_params=_params("parallel"),
        name="peer_route",
    )(q, keys, jnp.asarray(sel, BF16))


def _gelu_exact(x):
    return 0.5 * x * (1.0 + lax.erf(x * (1.0 / math.sqrt(2.0))))


PEER_SLOTS = 8
PEER_AHEAD = PEER_SLOTS - 1
IDX_PAD = SUBLANES
HI_HALF = -65536


def _bf16_bits_high(x):
    bits = lax.bitcast_convert_type(x, jnp.int32)
    return bits + 0x7FFF + jnp.bitwise_and(jnp.right_shift(bits, 16), 1)


def _pack_table_kernel(u_ref, v_ref, o_ref):
    nword = u_ref.shape[1] // (2 * LANES)
    for half, src in enumerate((u_ref, v_ref)):
        for j in range(nword):
            lo = _bf16_bits_high(src[:, 2 * j * LANES:(2 * j + 1) * LANES])
            hi = _bf16_bits_high(src[:, (2 * j + 1) * LANES:(2 * j + 2) * LANES])
            words = jnp.bitwise_or(jnp.bitwise_and(jnp.right_shift(lo, 16), 0xFFFF), jnp.bitwise_and(hi, HI_HALF))
            o_ref[:, half * nword + j, :] = words


def pack_expert_table(u, v):
    e, d = u.shape
    te = _row_tile(e, 256)
    return pl.pallas_call(
        _pack_table_kernel,
        grid=(e // te,),
        in_specs=[pl.BlockSpec((te, d), lambda i: (i, 0))] * 2,
        out_specs=pl.BlockSpec((te, d // LANES, LANES), lambda i: (i, 0, 0)),
        out_shape=jax.ShapeDtypeStruct((e, d // LANES, LANES), jnp.int32),
        compiler_params=_params("parallel"),
        name="pack_expert_table",
    )(u, v)


def _unpack_pair(words):
    lo = lax.bitcast_convert_type(jnp.left_shift(words, 16), F32)
    hi = lax.bitcast_convert_type(jnp.bitwise_and(words, HI_HALF), F32)
    return lo, hi


def _peer_expert_kernel(idx_hbm, gate_ref, h_ref, g_ref, fg_ref, tab_hbm, out_ref,
                        idx_smem, *scratch, final_norm):
    bufs = scratch[:PEER_SLOTS]
    xn_ref, idx_sems, row_sems = scratch[PEER_SLOTS:]
    tt, d = h_ref.shape
    nsel = gate_ref.shape[1]
    nword = d // (2 * LANES)
    per = nsel // (2 * nword)
    step = pl.program_id(0)
    nsteps = pl.num_programs(0)
    cur = lax.rem(step, 2)

    def idx_copy(s, slot):
        return pltpu.make_async_copy(idx_hbm.at[pl.ds(s * tt, tt + IDX_PAD)], idx_smem.at[slot], idx_sems.at[slot])

    @pl.when(step == 0)
    def _():
        idx_copy(0, 0).start()

    xn_ref[...] = _rms(h_ref[...], g_ref[...])
    idx_copy(step, cur).wait()

    @pl.when(step + 1 < nsteps)
    def _():
        idx_copy(step + 1, 1 - cur).start()

    def issue(trow, slot, k0, k1):
        for k in range(k0, k1):
            e = idx_smem[cur, trow, k]
            pltpu.make_async_copy(tab_hbm.at[e], bufs[slot].at[:, k], row_sems.at[slot]).start(priority=k % 2)

    def wait_rows(slot):
        pltpu.make_async_copy(bufs[slot], bufs[slot], row_sems.at[slot]).wait()

    @pl.when(step == 0)
    def _():
        for p in range(PEER_AHEAD):
            issue(p, p, 0, nsel)

    def token(t, s):
        nxt = (s + PEER_AHEAD) % PEER_SLOTS
        wait_rows(s)
        x = xn_ref[pl.ds(t, 1), :]
        acc = jnp.zeros((nsel, LANES), F32)
        for j in range(nword):
            lo, hi = _unpack_pair(bufs[s][j])
            acc = acc + lo * x[:, 2 * j * LANES:(2 * j + 1) * LANES]
            acc = acc + hi * x[:, (2 * j + 1) * LANES:(2 * j + 2) * LANES]
            issue(t + PEER_AHEAD, nxt, j * per, (j + 1) * per)
        act = jnp.sum(acc, axis=-1, keepdims=True)
        gate_col = jnp.broadcast_to(gate_ref[pl.ds(t, 1), :], (nsel, nsel)).T
        w = gate_col * _gelu_exact(act)
        pieces = []
        for j in range(nword):
            lo, hi = _unpack_pair(bufs[s][nword + j])
            pieces.append(jnp.sum(lo * w, axis=0, keepdims=True))
            pieces.append(jnp.sum(hi * w, axis=0, keepdims=True))
            issue(t + PEER_AHEAD, nxt, (nword + j) * per, (nword + j + 1) * per)
        out_ref[pl.ds(t, 1), :] = h_ref[pl.ds(t, 1), :] + jnp.concatenate(pieces, axis=1)

    def ring_turn(i, carry):
        for s in range(PEER_SLOTS):
            token(i * PEER_SLOTS + s, s)
        return carry

    lax.fori_loop(0, tt // PEER_SLOTS, ring_turn, 0)

    @pl.when(step == nsteps - 1)
    def _():
        for p in range(PEER_AHEAD):
            wait_rows(p)

    if final_norm:
        out_ref[...] = _rms(out_ref[...], fg_ref[...])


def peer_experts(h, gain, idx, gate, tab, final_gain=None):
    t, d = h.shape
    nsel = idx.shape[1]
    tt = _row_tile(t, 32)
    assert tt % PEER_SLOTS == 0 and PEER_AHEAD <= IDX_PAD and nsel == LANES
    final_norm = final_gain is not None
    fg = (final_gain if final_norm else gain).reshape(1, d)
    idx = jnp.pad(idx, ((0, IDX_PAD), (0, 0)))
    return pl.pallas_call(
        functools.partial(_peer_expert_kernel, final_norm=final_norm),
        grid=(t // tt,),
        in_specs=[pl.BlockSpec(memory_space=pl.ANY),
                  pl.BlockSpec((tt, nsel), lambda i: (i, 0)),
                  pl.BlockSpec((tt, d), lambda i: (i, 0)),
                  pl.BlockSpec((1, d), lambda i: (0, 0)),
                  pl.BlockSpec((1, d), lambda i: (0, 0)),
                  pl.BlockSpec(memory_space=pl.ANY)],
        out_specs=pl.BlockSpec((tt, d), lambda i: (i, 0)),
        out_shape=jax.ShapeDtypeStruct((t, d), F32),
        scratch_shapes=[pltpu.SMEM((2, tt + IDX_PAD, nsel), jnp.int32)]
        + [pltpu.VMEM(tab.shape[1:2] + (nsel, LANES), jnp.int32) for _ in range(PEER_SLOTS)]
        + [pltpu.VMEM((tt, d), F32),
           pltpu.SemaphoreType.DMA((2,)),
           pltpu.SemaphoreType.DMA((PEER_SLOTS,))],
        compiler_params=_params("arbitrary"),
        name="peer_experts",
    )(idx, gate, h, gain.reshape(1, d), fg, tab)


def peer_layer(h, gain, wq, keys, uv, final_gain=None):
    q = norm_matmul(h, gain, wq)
    idx, gate = peer_route(q, keys)
    return peer_experts(h, gain, idx, gate, uv, final_gain)


def kernel(x_prompt, x_sample, state_hgrn, cache_win_r1, cache_win_r4, cache_win_r16, norm_mix_a, w_in_a, lb_logits, gnorm_a, w_out_a, norm_kv, w_kv, norm_mix_b, w_q_b, w_out_b, norm_ffn, peer_wq, peer_keys, peer_u, peer_v, norm_final):
    batch, seq, d = x_prompt.shape
    dec = x_sample.shape[0]
    n_a = w_in_a.shape[0]
    depth = norm_ffn.shape[0]
    caches = (cache_win_r1, cache_win_r4, cache_win_r16)
    lb_all = jnp.cumsum(jax.nn.softmax(lb_logits.astype(F32), axis=0), axis=0)
    kw2 = 2 * B_KV_HEADS * B_HEAD_DIM

    hp = x_prompt.reshape(batch * seq, d)
    hs = x_sample.reshape(dec, d)
    states_p, states_s = [], []
    kv_p = kv_s = None
    for layer in range(depth):
        if layer < n_a:
            w_in = w_in_a[layer].astype(BF16)
            w_out = w_out_a[layer].astype(BF16)
            proj_p = norm_matmul(hp, norm_mix_a[layer], w_in)
            o_p, st_p = hgrn_prompt(proj_p, lb_all[layer], gnorm_a[layer], batch, seq)
            hp = matmul_residual(o_p, w_out, hp)
            proj_s = norm_matmul(hs, norm_mix_a[layer], w_in)
            o_s, st_s = hgrn_step(proj_s, lb_all[layer], gnorm_a[layer], state_hgrn[:, layer])
            hs = matmul_residual(o_s, w_out, hs)
            states_p.append(st_p)
            states_s.append(st_s)
        else:
            li = layer - n_a
            if layer == n_a:
                wkv = w_kv.astype(BF16)
                kv_p = norm_matmul(hp, norm_kv, wkv)
                kv_s = norm_matmul(hs, norm_kv, wkv)
            wq = w_q_b[li].astype(BF16)
            w_out = w_out_b[li].astype(BF16)
            q_p = norm_matmul(hp, norm_mix_b[li], wq)
            mix_p = dilated_prompt(q_p, kv_p, batch, seq)
            hp = matmul_residual(mix_p, w_out, hp)
            q_s = norm_matmul(hs, norm_mix_b[li], wq)
            mix_s = dilated_sample(q_s, kv_s, caches)
            hs = matmul_residual(mix_s, w_out, hs)
        uv = pack_expert_table(peer_u[layer], peer_v[layer])
        pwq = peer_wq[layer].astype(BF16)
        fin = norm_final if layer == depth - 1 else None
        hp = peer_layer(hp, norm_ffn[layer], pwq, peer_keys[layer], uv, fin)
        hs = peer_layer(hs, norm_ffn[layer], pwq, peer_keys[layer], uv, fin)

    kv_p6 = kv_p.reshape(batch, seq, N_BRANCHES, 2, B_KV_HEADS, B_HEAD_DIM)
    kv_s6 = kv_s.reshape(dec, 1, N_BRANCHES, 2, B_KV_HEADS, B_HEAD_DIM)
    rows_p = [kv_p6[:, seq - min(B_WINDOWS[z], seq):, z] for z in range(N_BRANCHES)]
    rows_s = [kv_s6[:, :, z] for z in range(N_BRANCHES)]
    return (hp.reshape(batch, seq, d), hs.reshape(dec, 1, d),
            jnp.stack(states_p, axis=1), rows_p[0], rows_p[1], rows_p[2],
            jnp.stack(states_s, axis=1), rows_s[0], rows_s[1], rows_s[2])
```

```python
import functools
import math

import jax
import jax.numpy as jnp
import numpy as np
from jax import lax
from jax.experimental import pallas as pl
from jax.experimental.pallas import tpu as pltpu

F32 = jnp.float32
BF16 = jnp.bfloat16
EPS = 1e-6

LANES = 128
SUBLANES = 8
VMEM_LIMIT = 56 * 1024 * 1024

A_HEADS = 16
A_DK = 128
A_CHUNK = 32
B_HEADS = 16
B_HEAD_DIM = 128
B_KV_HEADS = 4
B_WINDOWS = (128, 512, 2048)
B_DILATIONS = (1, 4, 16)
N_BRANCHES = 3
P_HEADS = 8
P_NKEYS = 128
P_TOPK = 16
P_HALF = 128
NEG_INF = float("-inf")


def _params(*sem):
    return pltpu.CompilerParams(dimension_semantics=sem, vmem_limit_bytes=VMEM_LIMIT)


def _dot(a, b):
    return jnp.dot(a, b, preferred_element_type=F32)


def _dot_nt(a, b):
    return lax.dot_general(a, b, (((1,), (1,)), ((), ())), preferred_element_type=F32)


def _dot_tn(a, b):
    return lax.dot_general(a, b, (((0,), (0,)), ((), ())), preferred_element_type=F32)


def _rms(x, g):
    ms = jnp.mean(x * x, axis=-1, keepdims=True)
    return x * lax.rsqrt(ms + EPS) * g


def _row_tile(m, want):
    t = min(m, want)
    while m % t:
        t //= 2
    return t


MM_ROWS = 1024
MM_COLS = 1024


def _norm_matmul_kernel(x_ref, g_ref, w_ref, o_ref, xn_ref):
    @pl.when(pl.program_id(1) == 0)
    def _():
        xn_ref[...] = _rms(x_ref[...], g_ref[...]).astype(BF16)

    o_ref[...] = _dot(xn_ref[...], w_ref[...])


def norm_matmul(x, g, w):
    m, k = x.shape
    n = w.shape[1]
    tm = _row_tile(m, MM_ROWS)
    tn = _row_tile(n, MM_COLS)
    return pl.pallas_call(
        _norm_matmul_kernel,
        grid=(m // tm, n // tn),
        in_specs=[pl.BlockSpec((tm, k), lambda i, j: (i, 0)),
                  pl.BlockSpec((1, k), lambda i, j: (0, 0)),
                  pl.BlockSpec((k, tn), lambda i, j: (0, j))],
        out_specs=pl.BlockSpec((tm, tn), lambda i, j: (i, j)),
        out_shape=jax.ShapeDtypeStruct((m, n), F32),
        scratch_shapes=[pltpu.VMEM((tm, k), BF16)],
        compiler_params=_params("parallel", "arbitrary"),
        name="norm_matmul",
    )(x, g.reshape(1, k), w)


def _matmul_res_kernel(a_ref, w_ref, r_ref, o_ref):
    o_ref[...] = r_ref[...] + _dot(a_ref[...], w_ref[...])


def matmul_residual(a, w, res):
    m, k = a.shape
    n = w.shape[1]
    tm = _row_tile(m, MM_ROWS)
    tn = _row_tile(n, MM_COLS)
    return pl.pallas_call(
        _matmul_res_kernel,
        grid=(m // tm, n // tn),
        in_specs=[pl.BlockSpec((tm, k), lambda i, j: (i, 0)),
                  pl.BlockSpec((k, tn), lambda i, j: (0, j)),
                  pl.BlockSpec((tm, tn), lambda i, j: (i, j))],
        out_specs=pl.BlockSpec((tm, tn), lambda i, j: (i, j)),
        out_shape=jax.ShapeDtypeStruct((m, n), F32),
        compiler_params=_params("parallel", "arbitrary"),
        name="matmul_residual",
    )(a, w, res)


def _split3(x):
    hi = x.astype(BF16)
    r1 = x - hi.astype(F32)
    mid = r1.astype(BF16)
    lo = (r1 - mid.astype(F32)).astype(BF16)
    return hi, mid, lo


def _gated_head_norm(o, gn, g):
    return _rms(o, gn) * jax.nn.sigmoid(g)


def _hgrn_prompt_kernel(q_ref, f_ref, i_ref, g_ref, lb_ref, gn_ref, o_ref, s_ref):
    seq = q_ref.shape[0]
    grp = LANES
    row = lax.broadcasted_iota(jnp.int32, (grp, grp), 0)
    col = lax.broadcasted_iota(jnp.int32, (grp, grp), 1)
    shift = A_CHUNK.bit_length() - 1
    same = jnp.right_shift(row, shift) == jnp.right_shift(col, shift)
    tri = same & (col <= row)
    l_tri = tri.astype(BF16)
    l_all = same.astype(BF16)
    gn = gn_ref[...]
    nhead = s_ref.shape[0]
    s_ref[...] = jnp.zeros(s_ref.shape, F32)

    def group(gi, carry):
        sl = pl.ds(pl.multiple_of(gi * grp, grp), grp)
        heads = range(nhead)
        cols = [slice(hh * A_DK, (hh + 1) * A_DK) for hh in heads]
        st = [s_ref[hh] for hh in heads]
        f = [lb_ref[:, cs] + (1.0 - lb_ref[:, cs]) * jax.nn.sigmoid(f_ref[sl, cs]) for cs in cols]
        k = [1.0 - fh for fh in f]
        parts = [_split3(jnp.log(fh)) for fh in f]
        b = [_dot(l_tri, p[0]) + _dot(l_tri, p[1]) + _dot(l_tri, p[2]) for p in parts]
        bt = [_dot(l_all, p[0]) + _dot(l_all, p[1]) + _dot(l_all, p[2]) for p in parts]
        qe = [(q_ref[sl, cs] * jnp.exp(bh)).astype(BF16) for cs, bh in zip(cols, b)]
        ke = [(kh * jnp.exp(-bh)).astype(BF16) for kh, bh in zip(k, b)]
        kd = [(kh * jnp.exp(bth - bh)).astype(BF16) for kh, bth, bh in zip(k, bt, b)]
        vb = [i_ref[sl, cs].astype(BF16) for cs in cols]
        att = [jnp.where(tri, _dot_nt(qh, kh), 0.0).astype(BF16) for qh, kh in zip(qe, ke)]
        o_intra = [_dot(ah, vh) for ah, vh in zip(att, vb)]
        dec_t = [jnp.exp(bth).T for bth in bt]
        outs = [[] for _ in heads]
        for c in range(grp // A_CHUNK):
            rows = slice(c * A_CHUNK, (c + 1) * A_CHUNK)
            for hh in heads:
                outs[hh].append(o_intra[hh][rows] + _dot(qe[hh][rows], st[hh].astype(BF16)))
            upd = [_dot_tn(kd[hh][rows], vb[hh][rows]) for hh in heads]
            st = [dec_t[hh][:, c * A_CHUNK:c * A_CHUNK + 1] * st[hh] + upd[hh] for hh in heads]
        for hh in heads:
            o = jnp.concatenate(outs[hh], axis=0)
            o_ref[sl, cols[hh]] = _gated_head_norm(o, gn, g_ref[sl, cols[hh]]).astype(BF16)
            s_ref[hh] = st[hh]
        return carry

    lax.fori_loop(0, seq // grp, group, 0)


HGRN_HEADS_PER_STEP = 4


def hgrn_prompt(proj, lb, gnorm, batch, seq):
    d = proj.shape[1] // 4
    dv = d // A_HEADS
    hb = HGRN_HEADS_PER_STEP
    ng = A_HEADS // hb

    def col(off):
        return pl.BlockSpec((seq, hb * dv), lambda b, h: (b, off * ng + h))

    return pl.pallas_call(
        _hgrn_prompt_kernel,
        grid=(batch, ng),
        in_specs=[col(0), col(1), col(2), col(3),
                  pl.BlockSpec((1, hb * A_DK), lambda b, h: (0, h)),
                  pl.BlockSpec((1, dv), lambda b, h: (0, 0))],
        out_specs=[pl.BlockSpec((seq, hb * dv), lambda b, h: (b, h)),
                   pl.BlockSpec((None, hb, A_DK, dv), lambda b, h: (b, h, 0, 0))],
        out_shape=[jax.ShapeDtypeStruct((batch * seq, d), BF16),
                   jax.ShapeDtypeStruct((batch, A_HEADS, A_DK, dv), F32)],
        compiler_params=_params("parallel", "parallel"),
        name="hgrn_prompt",
    )(proj, proj, proj, proj, lb.reshape(1, -1), gnorm.reshape(1, -1))


def _as_column(rowvec):
    n = rowvec.shape[1]
    return jnp.broadcast_to(rowvec, (n, n)).T


def _hgrn_step_kernel(p_ref, lb_ref, gn_ref, s_ref, o_ref, so_ref):
    nh = s_ref.shape[0]
    d = nh * A_DK
    for h in range(nh):
        cs = slice(h * A_DK, (h + 1) * A_DK)
        q = p_ref[:, h * A_DK:(h + 1) * A_DK]
        f_raw = p_ref[:, d + h * A_DK:d + (h + 1) * A_DK]
        v = p_ref[:, 2 * d + h * A_DK:2 * d + (h + 1) * A_DK]
        g = p_ref[:, 3 * d + h * A_DK:3 * d + (h + 1) * A_DK]
        f = lb_ref[:, cs] + (1.0 - lb_ref[:, cs]) * jax.nn.sigmoid(f_raw)
        s_new = _as_column(f) * s_ref[h] + _as_column(1.0 - f) * v
        so_ref[h] = s_new
        o = jnp.sum(_as_column(q) * s_new, axis=0, keepdims=True)
        o_ref[:, cs] = _gated_head_norm(o, gn_ref[...], g)


def hgrn_step(proj, lb, gnorm, state):
    bsz = proj.shape[0]
    d = proj.shape[1] // 4
    dv = d // A_HEADS
    assert dv == A_DK
    sspec = pl.BlockSpec((None, A_HEADS, A_DK, dv), lambda b: (b, 0, 0, 0))
    o, s_new = pl.pallas_call(
        _hgrn_step_kernel,
        grid=(bsz,),
        in_specs=[pl.BlockSpec((None, 1, 4 * d), lambda b: (b, 0, 0)),
                  pl.BlockSpec((1, d), lambda b: (0, 0)),
                  pl.BlockSpec((1, dv), lambda b: (0, 0)),
                  sspec],
        out_specs=[pl.BlockSpec((None, 1, d), lambda b: (b, 0, 0)), sspec],
        out_shape=[jax.ShapeDtypeStruct((bsz, 1, d), F32),
                   jax.ShapeDtypeStruct(state.shape, F32)],
        compiler_params=_params("parallel"),
        name="hgrn_step",
    )(proj.reshape(bsz, 1, 4 * d), lb.reshape(1, -1), gnorm.reshape(1, -1), state)
    return o.reshape(bsz, d).astype(BF16), s_new


def _alibi_slope(h):
    return 2.0 ** (-8.0 * (h + 1) / B_HEADS)


def _window_block(q_rows, keys, vals, slopes, bias_dist, valid):
    scale = B_HEAD_DIM ** -0.5
    s = [_dot_nt(qg.astype(BF16), keys) for qg in q_rows]
    s = [jnp.where(valid, sg * scale - slopes[:, g:g + 1] * bias_dist, NEG_INF) for g, sg in enumerate(s)]
    m = [jnp.max(sg, axis=-1, keepdims=True) for sg in s]
    p = [jnp.exp(sg - mg) for sg, mg in zip(s, m)]
    den = [jnp.sum(pg, axis=-1, keepdims=True) for pg in p]
    o = [_dot(pg.astype(BF16), vals) / dg for pg, dg in zip(p, den)]
    return [(og, mg + jnp.log(dg)) for og, mg, dg in zip(o, m, den)]


def _dilated_prompt_kernel(slope_ref, *refs):
    group = B_HEADS // B_KV_HEADS
    q_refs = refs[:group]
    kv_refs = refs[group:group + 2 * N_BRANCHES]
    out_ref, o_scr, lse_scr = refs[group + 2 * N_BRANCHES:]
    seq = out_ref.shape[0]
    slopes = slope_ref[...]

    for z in range(N_BRANCHES):
        dil = B_DILATIONS[z]
        steps = B_WINDOWS[z] // dil
        bq = min(steps, seq // dil)
        nblk = seq // (dil * bq)
        k_ref, v_ref = kv_refs[2 * z], kv_refs[2 * z + 1]
        lane = lax.broadcasted_iota(jnp.int32, (bq, LANES), 1)
        qa = lax.broadcasted_iota(jnp.int32, (bq, 2 * bq), 0)
        kb = lax.broadcasted_iota(jnp.int32, (bq, 2 * bq), 1)
        dist2 = bq + qa - kb
        near2 = (dist2 >= 0) & (dist2 <= steps)
        dist1 = dist2[:, bq:]
        near1 = near2[:, bq:]

        def store(rows, res):
            lse_tile = jnp.zeros((bq, LANES), F32)
            for g, (o, lse) in enumerate(res):
                o_scr[z, g, rows, :] = o
                lse_tile = jnp.where(lane == g, lse, lse_tile)
            lse_scr[z, rows, :] = lse_tile

        for c in range(dil):
            def rows_at(i):
                return pl.ds(c + i * (dil * bq), bq, stride=dil)

            first = rows_at(0)
            store(first, _window_block([q[first, :] for q in q_refs], k_ref[first, :].astype(BF16),
                                       v_ref[first, :].astype(BF16), slopes, (dist1 * dil).astype(F32), near1))

            def block(i, carry):
                cur, prev = rows_at(i), rows_at(i - 1)
                keys = jnp.concatenate([k_ref[prev, :], k_ref[cur, :]], axis=0).astype(BF16)
                vals = jnp.concatenate([v_ref[prev, :], v_ref[cur, :]], axis=0).astype(BF16)
                store(cur, _window_block([q[cur, :] for q in q_refs], keys, vals, slopes,
                                         (dist2 * dil).astype(F32), near2))
                return carry

            if nblk > 1:
                lax.fori_loop(1, nblk, block, 0)

    def mix(i, carry):
        rows = pl.ds(pl.multiple_of(i * LANES, LANES), LANES)
        l0, l1, l2 = lse_scr[0, rows, :], lse_scr[1, rows, :], lse_scr[2, rows, :]
        m = jnp.maximum(jnp.maximum(l0, l1), l2)
        e0, e1, e2 = jnp.exp(l0 - m), jnp.exp(l1 - m), jnp.exp(l2 - m)
        tot = e0 + e1 + e2
        w0, w1, w2 = e0 / tot, e1 / tot, e2 / tot
        for g in range(group):
            hs = slice(g * B_HEAD_DIM, (g + 1) * B_HEAD_DIM)
            out_ref[rows, hs] = (w0[:, g:g + 1] * o_scr[0, g, rows, :] + w1[:, g:g + 1] * o_scr[1, g, rows, :]
                                 + w2[:, g:g + 1] * o_scr[2, g, rows, :]).astype(BF16)
        return carry

    lax.fori_loop(0, seq // LANES, mix, 0)


def dilated_prompt(qp, kv, batch, seq):
    group = B_HEADS // B_KV_HEADS
    gw = group * B_HEAD_DIM
    assert seq % LANES == 0 and all(seq % min(B_WINDOWS[z], seq) == 0 for z in range(N_BRANCHES))
    slopes = jnp.exp2(-8.0 * jnp.arange(1, B_HEADS + 1, dtype=F32) / B_HEADS).reshape(B_KV_HEADS, group)
    slopes = jnp.pad(slopes, ((0, 0), (0, LANES - group))).reshape(B_KV_HEADS, 1, LANES)
    q_specs = [pl.BlockSpec((seq, B_HEAD_DIM), lambda b, kh, g=g: (b, kh * group + g)) for g in range(group)]
    kv_specs = []
    for z in range(N_BRANCHES):
        for part in range(2):
            kv_specs.append(pl.BlockSpec((seq, B_HEAD_DIM),
                                         lambda b, kh, z=z, part=part: (b, (2 * z + part) * B_KV_HEADS + kh)))
    return pl.pallas_call(
        _dilated_prompt_kernel,
        grid=(batch, B_KV_HEADS),
        in_specs=[pl.BlockSpec((None, 1, LANES), lambda b, kh: (kh, 0, 0))] + q_specs + kv_specs,
        out_specs=pl.BlockSpec((seq, gw), lambda b, kh: (b, kh)),
        out_shape=jax.ShapeDtypeStruct((batch * seq, B_HEADS * B_HEAD_DIM), BF16),
        scratch_shapes=[pltpu.VMEM((N_BRANCHES, group, seq, B_HEAD_DIM), F32),
                        pltpu.VMEM((N_BRANCHES, seq, LANES), F32)],
        compiler_params=_params("parallel", "parallel"),
        name="dilated_prompt",
    )(slopes, *([qp] * group), *([kv] * (2 * N_BRANCHES)))


def _dilated_sample_kernel(q_ref, new_ref, c0_ref, c1_ref, c2_ref, o_ref):
    caches = (c0_ref, c1_ref, c2_ref)
    nrow = c0_ref.shape[0]
    kw = B_KV_HEADS * B_HEAD_DIM
    scale = B_HEAD_DIM ** -0.5
    group = B_HEADS // B_KV_HEADS
    rowi = lax.broadcasted_iota(jnp.int32, (nrow, 1), 0)
    for h in range(B_HEADS):
        kh = h // group
        if h % group == 0:
            kcs = [caches[z][:, kh, :] for z in range(N_BRANCHES)]
            vcs = [caches[z][:, B_KV_HEADS + kh, :] for z in range(N_BRANCHES)]
        qh = q_ref[:, h * B_HEAD_DIM:(h + 1) * B_HEAD_DIM]
        outs, lses = [], []
        for z in range(N_BRANCHES):
            kc, vc = kcs[z], vcs[z]
            base = z * 2 * kw
            kn = new_ref[:, base + kh * B_HEAD_DIM:base + (kh + 1) * B_HEAD_DIM]
            vn = new_ref[:, base + kw + kh * B_HEAD_DIM:base + kw + (kh + 1) * B_HEAD_DIM]
            dist = ((nrow - rowi) * B_DILATIONS[z]).astype(F32)
            sc = jnp.sum(kc * qh, axis=-1, keepdims=True) * scale - _alibi_slope(h) * dist
            sn = jnp.sum(kn * qh, axis=-1, keepdims=True) * scale
            m = jnp.maximum(jnp.max(sc, axis=0, keepdims=True), sn)
            pc = jnp.exp(sc - m)
            pn = jnp.exp(sn - m)
            den = jnp.sum(pc, axis=0, keepdims=True) + pn
            outs.append((jnp.sum(pc * vc, axis=0, keepdims=True) + pn * vn) / den)
            lses.append(m + jnp.log(den))
        mm = jnp.maximum(jnp.maximum(lses[0], lses[1]), lses[2])
        es = [jnp.exp(l - mm) for l in lses]
        tot = es[0] + es[1] + es[2]
        mix = (es[0] / tot) * outs[0] + (es[1] / tot) * outs[1] + (es[2] / tot) * outs[2]
        o_ref[:, h * B_HEAD_DIM:(h + 1) * B_HEAD_DIM] = mix


def dilated_sample(qs, kv_new, caches):
    bsz, hd = qs.shape
    in_specs = [pl.BlockSpec((None, 1, hd), lambda b: (b, 0, 0)),
                pl.BlockSpec((None, 1, kv_new.shape[1]), lambda b: (b, 0, 0))]
    args = [qs.reshape(bsz, 1, hd), kv_new.reshape(bsz, 1, -1)]
    for z, cache in enumerate(caches):
        dil = B_DILATIONS[z]
        steps = B_WINDOWS[z] // dil
        assert cache.shape[1] == steps * dil
        args.append(cache.reshape(bsz, steps, dil, 2 * B_KV_HEADS, B_HEAD_DIM))
        in_specs.append(pl.BlockSpec((None, steps, None, 2 * B_KV_HEADS, B_HEAD_DIM), lambda b: (b, 0, 0, 0, 0)))
    out = pl.pallas_call(
        _dilated_sample_kernel,
        grid=(bsz,),
        in_specs=in_specs,
        out_specs=pl.BlockSpec((None, 1, hd), lambda b: (b, 0, 0)),
        out_shape=jax.ShapeDtypeStruct((bsz, 1, hd), F32),
        compiler_params=_params("parallel"),
        name="dilated_sample",
    )(*args)
    return out.reshape(bsz, hd).astype(BF16)


def _topk_rows(x, payload=None):
    r = x.shape[0]
    iota = lax.broadcasted_iota(jnp.int32, x.shape, 0).astype(F32)
    vals, idxs = [], []
    for _ in range(P_TOPK):
        m = jnp.max(x, axis=0, keepdims=True)
        am = jnp.min(jnp.where(x == m, iota, float(r)), axis=0, keepdims=True)
        hit = iota == am
        vals.append(m)
        if payload is None:
            idxs.append(am)
        else:
            idxs.append(jnp.max(jnp.where(hit, payload, 0.0), axis=0, keepdims=True))
        x = jnp.where(hit, NEG_INF, x)
    return jnp.concatenate(vals, axis=0), jnp.concatenate(idxs, axis=0)


def _staircase_pairs():
    return [(a, b) for a in range(P_TOPK) for b in range(P_TOPK) if (a + 1) * (b + 1) <= P_TOPK]


def _pick_rows(sel, x):
    hi, mid, lo = _split3(x)
    return _dot(sel, hi) + _dot(sel, mid) + _dot(sel, lo)


def _peer_route_kernel(q_ref, keys_ref, sel_ref, idx_ref, gate_ref, *, npair):
    sel_a, sel_b = sel_ref[0], sel_ref[1]
    nrow = sel_a.shape[0]
    pad = jnp.where(lax.broadcasted_iota(jnp.int32, (nrow, 1), 0) < npair, 0.0, NEG_INF)
    idx_rows, gate_rows = [], []
    for h in range(P_HEADS):
        sub = []
        for c in range(2):
            off = (h * 2 + c) * P_HALF
            qhc = q_ref[:, off:off + P_HALF].astype(BF16)
            s_t = _dot_nt(keys_ref[h, c].astype(BF16), qhc)
            sub.append(_topk_rows(s_t))
        (sv0, si0), (sv1, si1) = sub
        cand = _pick_rows(sel_a, sv0) + _pick_rows(sel_b, sv1) + pad
        cidx = _dot(sel_a, si0.astype(BF16)) * P_NKEYS + _dot(sel_b, si1.astype(BF16))
        top_s, eidx = _topk_rows(cand, cidx)
        e = jnp.exp(top_s - top_s[0:1])
        gate_rows.append(e / jnp.sum(e, axis=0, keepdims=True))
        idx_rows.append(eidx)
    idx_ref[...] = jnp.concatenate(idx_rows, axis=0).T.astype(jnp.int32)
    gate_ref[...] = jnp.concatenate(gate_rows, axis=0).T


def peer_route(q, keys):
    t, qd = q.shape
    tt = _row_tile(t, LANES)
    nsel = P_HEADS * P_TOPK
    pairs = _staircase_pairs()
    nrow = -(-len(pairs) // SUBLANES) * SUBLANES
    sel = np.zeros((2, nrow, P_TOPK), np.float32)
    for r, (a, b) in enumerate(pairs):
        sel[0, r, a] = 1.0
        sel[1, r, b] = 1.0
    return pl.pallas_call(
        functools.partial(_peer_route_kernel, npair=len(pairs)),
        grid=(t // tt,),
        in_specs=[pl.BlockSpec((tt, qd), lambda i: (i, 0)),
                  pl.BlockSpec(keys.shape, lambda i: (0, 0, 0, 0)),
                  pl.BlockSpec(sel.shape, lambda i: (0, 0, 0))],
        out_specs=[pl.BlockSpec((tt, nsel), lambda i: (i, 0)),
                   pl.BlockSpec((tt, nsel), lambda i: (i, 0))],
        out_shape=[jax.ShapeDtypeStruct((t, nsel), jnp.int32),
                   jax.ShapeDtypeStruct((t, nsel), F32)],
        compiler_params=_params("parallel"),
        name="peer_route",
    )(q, keys, jnp.asarray(sel, BF16))


def _gelu_exact(x):
    return 0.5 * x * (1.0 + lax.erf(x * (1.0 / math.sqrt(2.0))))


PEER_SLOTS = 8
PEER_AHEAD = PEER_SLOTS - 1
IDX_PAD = SUBLANES
HI_HALF = -65536


def _bf16_bits_high(x):
    bits = lax.bitcast_convert_type(x, jnp.int32)
    return bits + 0x7FFF + jnp.bitwise_and(jnp.right_shift(bits, 16), 1)


def _pack_table_kernel(u_ref, v_ref, o_ref):
    nword = u_ref.shape[1] // (2 * LANES)
    for half, src in enumerate((u_ref, v_ref)):
        for j in range(nword):
            lo = _bf16_bits_high(src[:, 2 * j * LANES:(2 * j + 1) * LANES])
            hi = _bf16_bits_high(src[:, (2 * j + 1) * LANES:(2 * j + 2) * LANES])
            words = jnp.bitwise_or(jnp.bitwise_and(jnp.right_shift(lo, 16), 0xFFFF), jnp.bitwise_and(hi, HI_HALF))
            o_ref[:, half * nword + j, :] = words


def pack_expert_table(u, v, layer):
    _, e, d = u.shape
    te = _row_tile(e, 256)
    return pl.pallas_call(
        _pack_table_kernel,
        grid=(e // te,),
        in_specs=[pl.BlockSpec((None, te, d), lambda i: (layer, i, 0))] * 2,
        out_specs=pl.BlockSpec((te, d // LANES, LANES), lambda i: (i, 0, 0)),
        out_shape=jax.ShapeDtypeStruct((e, d // LANES, LANES), jnp.int32),
        compiler_params=_params("parallel"),
        name="pack_expert_table",
    )(u, v)


def _unpack_pair(words):
    lo = lax.bitcast_convert_type(jnp.left_shift(words, 16), F32)
    hi = lax.bitcast_convert_type(jnp.bitwise_and(words, HI_HALF), F32)
    return lo, hi


def _peer_expert_kernel(idx_hbm, gate_ref, h_ref, g_ref, fg_ref, tab_hbm, out_ref,
                        idx_smem, *scratch, final_norm):
    bufs = scratch[:PEER_SLOTS]
    xn_ref, idx_sems, row_sems = scratch[PEER_SLOTS:]
    tt, d = h_ref.shape
    nsel = gate_ref.shape[1]
    nword = d // (2 * LANES)
    per = nsel // (2 * nword)
    step = pl.program_id(0)
    nsteps = pl.num_programs(0)
    cur = lax.rem(step, 2)

    def idx_copy(s, slot):
        return pltpu.make_async_copy(idx_hbm.at[pl.ds(s * tt, tt + IDX_PAD)], idx_smem.at[slot], idx_sems.at[slot])

    @pl.when(step == 0)
    def _():
        idx_copy(0, 0).start()

    xn_ref[...] = _rms(h_ref[...], g_ref[...])
    idx_copy(step, cur).wait()

    @pl.when(step + 1 < nsteps)
    def _():
        idx_copy(step + 1, 1 - cur).start()

    def issue(trow, slot, k0, k1):
        for k in range(k0, k1):
            e = idx_smem[cur, trow, k]
            pltpu.make_async_copy(tab_hbm.at[e], bufs[slot].at[:, k], row_sems.at[slot]).start(priority=k % 2)

    def wait_rows(slot):
        pltpu.make_async_copy(bufs[slot], bufs[slot], row_sems.at[slot]).wait()

    @pl.when(step == 0)
    def _():
        for p in range(PEER_AHEAD):
            issue(p, p, 0, nsel)

    def token(t, s):
        nxt = (s + PEER_AHEAD) % PEER_SLOTS
        wait_rows(s)
        x = xn_ref[pl.ds(t, 1), :]
        acc = jnp.zeros((nsel, LANES), F32)
        for j in range(nword):
            lo, hi = _unpack_pair(bufs[s][j])
            acc = acc + lo * x[:, 2 * j * LANES:(2 * j + 1) * LANES]
            acc = acc + hi * x[:, (2 * j + 1) * LANES:(2 * j + 2) * LANES]
            issue(t + PEER_AHEAD, nxt, j * per, (j + 1) * per)
        act = jnp.sum(acc, axis=-1, keepdims=True)
        gate_col = jnp.broadcast_to(gate_ref[pl.ds(t, 1), :], (nsel, nsel)).T
        w = gate_col * _gelu_exact(act)
        pieces = []
        for j in range(nword):
            lo, hi = _unpack_pair(bufs[s][nword + j])
            pieces.append(jnp.sum(lo * w, axis=0, keepdims=True))
            pieces.append(jnp.sum(hi * w, axis=0, keepdims=True))
            issue(t + PEER_AHEAD, nxt, (nword + j) * per, (nword + j + 1) * per)
        out_ref[pl.ds(t, 1), :] = h_ref[pl.ds(t, 1), :] + jnp.concatenate(pieces, axis=1)

    def ring_turn(i, carry):
        for s in range(PEER_SLOTS):
            token(i * PEER_SLOTS + s, s)
        return carry

    lax.fori_loop(0, tt // PEER_SLOTS, ring_turn, 0)

    @pl.when(step == nsteps - 1)
    def _():
        for p in range(PEER_AHEAD):
            wait_rows(p)

    if final_norm:
        out_ref[...] = _rms(out_ref[...], fg_ref[...])


def peer_experts(h, gain, idx, gate, tab, final_gain=None):
    t, d = h.shape
    nsel = idx.shape[1]
    tt = _row_tile(t, 32)
    assert tt % PEER_SLOTS == 0 and PEER_AHEAD <= IDX_PAD and nsel == LANES
    final_norm = final_gain is not None
    fg = (final_gain if final_norm else gain).reshape(1, d)
    idx = jnp.pad(idx, ((0, IDX_PAD), (0, 0)))
    return pl.pallas_call(
        functools.partial(_peer_expert_kernel, final_norm=final_norm),
        grid=(t // tt,),
        in_specs=[pl.BlockSpec(memory_space=pl.ANY),
                  pl.BlockSpec((tt, nsel), lambda i: (i, 0)),
                  pl.BlockSpec((tt, d), lambda i: (i, 0)),
                  pl.BlockSpec((1, d), lambda i: (0, 0)),
                  pl.BlockSpec((1, d), lambda i: (0, 0)),
                  pl.BlockSpec(memory_space=pl.ANY)],
        out_specs=pl.BlockSpec((tt, d), lambda i: (i, 0)),
        out_shape=jax.ShapeDtypeStruct((t, d), F32),
        scratch_shapes=[pltpu.SMEM((2, tt + IDX_PAD, nsel), jnp.int32)]
        + [pltpu.VMEM(tab.shape[1:2] + (nsel, LANES), jnp.int32) for _ in range(PEER_SLOTS)]
        + [pltpu.VMEM((tt, d), F32),
           pltpu.SemaphoreType.DMA((2,)),
           pltpu.SemaphoreType.DMA((PEER_SLOTS,))],
        compiler_params=_params("arbitrary"),
        name="peer_experts",
    )(idx, gate, h, gain.reshape(1, d), fg, tab)


def peer_layer(h, gain, wq, keys, uv, final_gain=None):
    q = norm_matmul(h, gain, wq)
    idx, gate = peer_route(q, keys)
    return peer_experts(h, gain, idx, gate, uv, final_gain)


def kernel(x_prompt, x_sample, state_hgrn, cache_win_r1, cache_win_r4, cache_win_r16, norm_mix_a, w_in_a, lb_logits, gnorm_a, w_out_a, norm_kv, w_kv, norm_mix_b, w_q_b, w_out_b, norm_ffn, peer_wq, peer_keys, peer_u, peer_v, norm_final):
    batch, seq, d = x_prompt.shape
    dec = x_sample.shape[0]
    n_a = w_in_a.shape[0]
    depth = norm_ffn.shape[0]
    caches = (cache_win_r1, cache_win_r4, cache_win_r16)
    lb_all = jnp.cumsum(jax.nn.softmax(lb_logits.astype(F32), axis=0), axis=0)
    kw2 = 2 * B_KV_HEADS * B_HEAD_DIM

    hp = x_prompt.reshape(batch * seq, d)
    hs = x_sample.reshape(dec, d)
    states_p, states_s = [], []
    kv_p = kv_s = None
    for layer in range(depth):
        if layer < n_a:
            w_in = w_in_a[layer].astype(BF16)
            w_out = w_out_a[layer].astype(BF16)
            proj_p = norm_matmul(hp, norm_mix_a[layer], w_in)
            o_p, st_p = hgrn_prompt(proj_p, lb_all[layer], gnorm_a[layer], batch, seq)
            hp = matmul_residual(o_p, w_out, hp)
            proj_s = norm_matmul(hs, norm_mix_a[layer], w_in)
            o_s, st_s = hgrn_step(proj_s, lb_all[layer], gnorm_a[layer], state_hgrn[:, layer])
            hs = matmul_residual(o_s, w_out, hs)
            states_p.append(st_p)
            states_s.append(st_s)
        else:
            li = layer - n_a
            if layer == n_a:
                wkv = w_kv.astype(BF16)
                kv_p = norm_matmul(hp, norm_kv, wkv)
                kv_s = norm_matmul(hs, norm_kv, wkv)
            wq = w_q_b[li].astype(BF16)
            w_out = w_out_b[li].astype(BF16)
            q_p = norm_matmul(hp, norm_mix_b[li], wq)
            mix_p = dilated_prompt(q_p, kv_p, batch, seq)
            hp = matmul_residual(mix_p, w_out, hp)
            q_s = norm_matmul(hs, norm_mix_b[li], wq)
            mix_s = dilated_sample(q_s, kv_s, caches)
            hs = matmul_residual(mix_s, w_out, hs)
        uv = pack_expert_table(peer_u, peer_v, layer)
        pwq = peer_wq[layer].astype(BF16)
        fin = norm_final if layer == depth - 1 else None
        hp = peer_layer(hp, norm_ffn[layer], pwq, peer_keys[layer], uv, fin)
        hs = peer_layer(hs, norm_ffn[layer], pwq, peer_keys[layer], uv, fin)

    def branch_rows(kv, nb, t, z, keep):
        rows = kv.reshape(nb, t, N_BRANCHES * kw2)[:, t - keep:, z * kw2:(z + 1) * kw2]
        return rows.reshape(nb, keep, 2, B_KV_HEADS, B_HEAD_DIM)

    rows_p = [branch_rows(kv_p, batch, seq, z, min(B_WINDOWS[z], seq)) for z in range(N_BRANCHES)]
    rows_s = [branch_rows(kv_s, dec, 1, z, 1) for z in range(N_BRANCHES)]
    return (hp.reshape(batch, seq, d), hs.reshape(dec, 1, d),
            jnp.stack(states_p, axis=1), rows_p[0], rows_p[1], rows_p[2],
            jnp.stack(states_s, axis=1), rows_s[0], rows_s[1], rows_s[2])
```

```python
import functools
import math

import jax
import jax.numpy as jnp
import numpy as np
from jax import lax
from jax.experimental import pallas as pl
from jax.experimental.pallas import tpu as pltpu

F32 = jnp.float32
BF16 = jnp.bfloat16
EPS = 1e-6

LANES = 128
SUBLANES = 8
VMEM_LIMIT = 56 * 1024 * 1024

A_HEADS = 16
A_DK = 128
A_CHUNK = 32
B_HEADS = 16
B_HEAD_DIM = 128
B_KV_HEADS = 4
B_WINDOWS = (128, 512, 2048)
B_DILATIONS = (1, 4, 16)
N_BRANCHES = 3
P_HEADS = 8
P_NKEYS = 128
P_TOPK = 16
P_HALF = 128
NEG_INF = float("-inf")


def _params(*sem):
    return pltpu.CompilerParams(dimension_semantics=sem, vmem_limit_bytes=VMEM_LIMIT)


def _dot(a, b):
    return jnp.dot(a, b, preferred_element_type=F32)


def _dot_nt(a, b):
    return lax.dot_general(a, b, (((1,), (1,)), ((), ())), preferred_element_type=F32)


def _dot_tn(a, b):
    return lax.dot_general(a, b, (((0,), (0,)), ((), ())), preferred_element_type=F32)


def _rms(x, g):
    ms = jnp.mean(x * x, axis=-1, keepdims=True)
    return x * lax.rsqrt(ms + EPS) * g


def _row_tile(m, want):
    t = min(m, want)
    while m % t:
        t //= 2
    return t


MM_ROWS = 1024
MM_COLS = 1024


def _norm_matmul_kernel(x_ref, g_ref, w_ref, o_ref, xn_ref):
    @pl.when(pl.program_id(1) == 0)
    def _():
        xn_ref[...] = _rms(x_ref[...], g_ref[...]).astype(BF16)

    o_ref[...] = _dot(xn_ref[...], w_ref[...])


def norm_matmul(x, g, w):
    m, k = x.shape
    n = w.shape[1]
    tm = _row_tile(m, MM_ROWS)
    tn = _row_tile(n, MM_COLS)
    return pl.pallas_call(
        _norm_matmul_kernel,
        grid=(m // tm, n // tn),
        in_specs=[pl.BlockSpec((tm, k), lambda i, j: (i, 0)),
                  pl.BlockSpec((1, k), lambda i, j: (0, 0)),
                  pl.BlockSpec((k, tn), lambda i, j: (0, j))],
        out_specs=pl.BlockSpec((tm, tn), lambda i, j: (i, j)),
        out_shape=jax.ShapeDtypeStruct((m, n), F32),
        scratch_shapes=[pltpu.VMEM((tm, k), BF16)],
        compiler_params=_params("parallel", "arbitrary"),
        name="norm_matmul",
    )(x, g.reshape(1, k), w)


def _matmul_res_kernel(a_ref, w_ref, r_ref, o_ref):
    o_ref[...] = r_ref[...] + _dot(a_ref[...], w_ref[...])


def matmul_residual(a, w, res):
    m, k = a.shape
    n = w.shape[1]
    tm = _row_tile(m, MM_ROWS)
    tn = _row_tile(n, MM_COLS)
    return pl.pallas_call(
        _matmul_res_kernel,
        grid=(m // tm, n // tn),
        in_specs=[pl.BlockSpec((tm, k), lambda i, j: (i, 0)),
                  pl.BlockSpec((k, tn), lambda i, j: (0, j)),
                  pl.BlockSpec((tm, tn), lambda i, j: (i, j))],
        out_specs=pl.BlockSpec((tm, tn), lambda i, j: (i, j)),
        out_shape=jax.ShapeDtypeStruct((m, n), F32),
        compiler_params=_params("parallel", "arbitrary"),
        name="matmul_residual",
    )(a, w, res)


def _split3(x):
    hi = x.astype(BF16)
    r1 = x - hi.astype(F32)
    mid = r1.astype(BF16)
    lo = (r1 - mid.astype(F32)).astype(BF16)
    return hi, mid, lo


def _gated_head_norm(o, gn, g):
    return _rms(o, gn) * jax.nn.sigmoid(g)


def _hgrn_prompt_kernel(q_ref, f_ref, i_ref, g_ref, lb_ref, gn_ref, o_ref, s_ref):
    seq = q_ref.shape[0]
    grp = LANES
    row = lax.broadcasted_iota(jnp.int32, (grp, grp), 0)
    col = lax.broadcasted_iota(jnp.int32, (grp, grp), 1)
    shift = A_CHUNK.bit_length() - 1
    same = jnp.right_shift(row, shift) == jnp.right_shift(col, shift)
    tri = same & (col <= row)
    l_tri = tri.astype(BF16)
    l_all = same.astype(BF16)
    gn = gn_ref[...]
    nhead = s_ref.shape[0]
    s_ref[...] = jnp.zeros(s_ref.shape, F32)

    def group(gi, carry):
        sl = pl.ds(pl.multiple_of(gi * grp, grp), grp)
        heads = range(nhead)
        cols = [slice(hh * A_DK, (hh + 1) * A_DK) for hh in heads]
        st = [s_ref[hh] for hh in heads]
        f = [lb_ref[:, cs] + (1.0 - lb_ref[:, cs]) * jax.nn.sigmoid(f_ref[sl, cs]) for cs in cols]
        k = [1.0 - fh for fh in f]
        parts = [_split3(jnp.log(fh)) for fh in f]
        b = [_dot(l_tri, p[0]) + _dot(l_tri, p[1]) + _dot(l_tri, p[2]) for p in parts]
        bt = [_dot(l_all, p[0]) + _dot(l_all, p[1]) + _dot(l_all, p[2]) for p in parts]
        qe = [(q_ref[sl, cs] * jnp.exp(bh)).astype(BF16) for cs, bh in zip(cols, b)]
        ke = [(kh * jnp.exp(-bh)).astype(BF16) for kh, bh in zip(k, b)]
        kd = [(kh * jnp.exp(bth - bh)).astype(BF16) for kh, bth, bh in zip(k, bt, b)]
        vb = [i_ref[sl, cs].astype(BF16) for cs in cols]
        att = [jnp.where(tri, _dot_nt(qh, kh), 0.0).astype(BF16) for qh, kh in zip(qe, ke)]
        o_intra = [_dot(ah, vh) for ah, vh in zip(att, vb)]
        dec_t = [jnp.exp(bth).T for bth in bt]
        outs = [[] for _ in heads]
        for c in range(grp // A_CHUNK):
            rows = slice(c * A_CHUNK, (c + 1) * A_CHUNK)
            for hh in heads:
                outs[hh].append(o_intra[hh][rows] + _dot(qe[hh][rows], st[hh].astype(BF16)))
            upd = [_dot_tn(kd[hh][rows], vb[hh][rows]) for hh in heads]
            st = [dec_t[hh][:, c * A_CHUNK:c * A_CHUNK + 1] * st[hh] + upd[hh] for hh in heads]
        for hh in heads:
            o = jnp.concatenate(outs[hh], axis=0)
            o_ref[sl, cols[hh]] = _gated_head_norm(o, gn, g_ref[sl, cols[hh]]).astype(BF16)
            s_ref[hh] = st[hh]
        return carry

    lax.fori_loop(0, seq // grp, group, 0)


HGRN_HEADS_PER_STEP = 4


def hgrn_prompt(proj, lb, gnorm, batch, seq):
    d = proj.shape[1] // 4
    dv = d // A_HEADS
    hb = HGRN_HEADS_PER_STEP
    ng = A_HEADS // hb

    def col(off):
        return pl.BlockSpec((seq, hb * dv), lambda b, h: (b, off * ng + h))

    return pl.pallas_call(
        _hgrn_prompt_kernel,
        grid=(batch, ng),
        in_specs=[col(0), col(1), col(2), col(3),
                  pl.BlockSpec((1, hb * A_DK), lambda b, h: (0, h)),
                  pl.BlockSpec((1, dv), lambda b, h: (0, 0))],
        out_specs=[pl.BlockSpec((seq, hb * dv), lambda b, h: (b, h)),
                   pl.BlockSpec((None, hb, A_DK, dv), lambda b, h: (b, h, 0, 0))],
        out_shape=[jax.ShapeDtypeStruct((batch * seq, d), BF16),
                   jax.ShapeDtypeStruct((batch, A_HEADS, A_DK, dv), F32)],
        compiler_params=_params("parallel", "parallel"),
        name="hgrn_prompt",
    )(proj, proj, proj, proj, lb.reshape(1, -1), gnorm.reshape(1, -1))


def _as_column(rowvec):
    n = rowvec.shape[1]
    return jnp.broadcast_to(rowvec, (n, n)).T


def _hgrn_step_kernel(p_ref, lb_ref, gn_ref, s_ref, o_ref, so_ref):
    nh = s_ref.shape[0]
    d = nh * A_DK
    for h in range(nh):
        cs = slice(h * A_DK, (h + 1) * A_DK)
        q = p_ref[:, h * A_DK:(h + 1) * A_DK]
        f_raw = p_ref[:, d + h * A_DK:d + (h + 1) * A_DK]
        v = p_ref[:, 2 * d + h * A_DK:2 * d + (h + 1) * A_DK]
        g = p_ref[:, 3 * d + h * A_DK:3 * d + (h + 1) * A_DK]
        f = lb_ref[:, cs] + (1.0 - lb_ref[:, cs]) * jax.nn.sigmoid(f_raw)
        s_new = _as_column(f) * s_ref[h] + _as_column(1.0 - f) * v
        so_ref[h] = s_new
        o = jnp.sum(_as_column(q) * s_new, axis=0, keepdims=True)
        o_ref[:, cs] = _gated_head_norm(o, gn_ref[...], g)


def hgrn_step(proj, lb, gnorm, state):
    bsz = proj.shape[0]
    d = proj.shape[1] // 4
    dv = d // A_HEADS
    assert dv == A_DK
    sspec = pl.BlockSpec((None, A_HEADS, A_DK, dv), lambda b: (b, 0, 0, 0))
    o, s_new = pl.pallas_call(
        _hgrn_step_kernel,
        grid=(bsz,),
        in_specs=[pl.BlockSpec((None, 1, 4 * d), lambda b: (b, 0, 0)),
                  pl.BlockSpec((1, d), lambda b: (0, 0)),
                  pl.BlockSpec((1, dv), lambda b: (0, 0)),
                  sspec],
        out_specs=[pl.BlockSpec((None, 1, d), lambda b: (b, 0, 0)), sspec],
        out_shape=[jax.ShapeDtypeStruct((bsz, 1, d), F32),
                   jax.ShapeDtypeStruct(state.shape, F32)],
        compiler_params=_params("parallel"),
        name="hgrn_step",
    )(proj.reshape(bsz, 1, 4 * d), lb.reshape(1, -1), gnorm.reshape(1, -1), state)
    return o.reshape(bsz, d).astype(BF16), s_new


def _alibi_slope(h):
    return 2.0 ** (-8.0 * (h + 1) / B_HEADS)


def _window_block(q_rows, keys, vals, slopes, bias_dist, valid):
    scale = B_HEAD_DIM ** -0.5
    s = [_dot_nt(qg.astype(BF16), keys) for qg in q_rows]
    s = [jnp.where(valid, sg * scale - slopes[:, g:g + 1] * bias_dist, NEG_INF) for g, sg in enumerate(s)]
    m = [jnp.max(sg, axis=-1, keepdims=True) for sg in s]
    p = [jnp.exp(sg - mg) for sg, mg in zip(s, m)]
    den = [jnp.sum(pg, axis=-1, keepdims=True) for pg in p]
    o = [_dot(pg.astype(BF16), vals) / dg for pg, dg in zip(p, den)]
    return [(og, mg + jnp.log(dg)) for og, mg, dg in zip(o, m, den)]


def _dilated_prompt_kernel(slope_ref, *refs):
    group = B_HEADS // B_KV_HEADS
    q_refs = refs[:group]
    kv_refs = refs[group:group + 2 * N_BRANCHES]
    out_ref, o_scr, lse_scr = refs[group + 2 * N_BRANCHES:]
    seq = out_ref.shape[0]
    slopes = slope_ref[...]

    for z in range(N_BRANCHES):
        dil = B_DILATIONS[z]
        steps = B_WINDOWS[z] // dil
        bq = min(steps, seq // dil)
        nblk = seq // (dil * bq)
        k_ref, v_ref = kv_refs[2 * z], kv_refs[2 * z + 1]
        lane = lax.broadcasted_iota(jnp.int32, (bq, LANES), 1)
        qa = lax.broadcasted_iota(jnp.int32, (bq, 2 * bq), 0)
        kb = lax.broadcasted_iota(jnp.int32, (bq, 2 * bq), 1)
        dist2 = bq + qa - kb
        near2 = (dist2 >= 0) & (dist2 <= steps)
        dist1 = dist2[:, bq:]
        near1 = near2[:, bq:]

        def store(rows, res):
            lse_tile = jnp.zeros((bq, LANES), F32)
            for g, (o, lse) in enumerate(res):
                o_scr[z, g, rows, :] = o
                lse_tile = jnp.where(lane == g, lse, lse_tile)
            lse_scr[z, rows, :] = lse_tile

        for c in range(dil):
            def rows_at(i):
                return pl.ds(c + i * (dil * bq), bq, stride=dil)

            first = rows_at(0)
            store(first, _window_block([q[first, :] for q in q_refs], k_ref[first, :].astype(BF16),
                                       v_ref[first, :].astype(BF16), slopes, (dist1 * dil).astype(F32), near1))

            def block(i, carry):
                cur, prev = rows_at(i), rows_at(i - 1)
                keys = jnp.concatenate([k_ref[prev, :], k_ref[cur, :]], axis=0).astype(BF16)
                vals = jnp.concatenate([v_ref[prev, :], v_ref[cur, :]], axis=0).astype(BF16)
                store(cur, _window_block([q[cur, :] for q in q_refs], keys, vals, slopes,
                                         (dist2 * dil).astype(F32), near2))
                return carry

            if nblk > 1:
                lax.fori_loop(1, nblk, block, 0)

    def mix(i, carry):
        rows = pl.ds(pl.multiple_of(i * LANES, LANES), LANES)
        l0, l1, l2 = lse_scr[0, rows, :], lse_scr[1, rows, :], lse_scr[2, rows, :]
        m = jnp.maximum(jnp.maximum(l0, l1), l2)
        e0, e1, e2 = jnp.exp(l0 - m), jnp.exp(l1 - m), jnp.exp(l2 - m)
        tot = e0 + e1 + e2
        w0, w1, w2 = e0 / tot, e1 / tot, e2 / tot
        for g in range(group):
            hs = slice(g * B_HEAD_DIM, (g + 1) * B_HEAD_DIM)
            out_ref[rows, hs] = (w0[:, g:g + 1] * o_scr[0, g, rows, :] + w1[:, g:g + 1] * o_scr[1, g, rows, :]
                                 + w2[:, g:g + 1] * o_scr[2, g, rows, :]).astype(BF16)
        return carry

    lax.fori_loop(0, seq // LANES, mix, 0)


def dilated_prompt(qp, kv, batch, seq):
    group = B_HEADS // B_KV_HEADS
    gw = group * B_HEAD_DIM
    assert seq % LANES == 0 and all(seq % min(B_WINDOWS[z], seq) == 0 for z in range(N_BRANCHES))
    slopes = jnp.exp2(-8.0 * jnp.arange(1, B_HEADS + 1, dtype=F32) / B_HEADS).reshape(B_KV_HEADS, group)
    slopes = jnp.pad(slopes, ((0, 0), (0, LANES - group))).reshape(B_KV_HEADS, 1, LANES)
    q_specs = [pl.BlockSpec((seq, B_HEAD_DIM), lambda b, kh, g=g: (b, kh * group + g)) for g in range(group)]
    kv_specs = []
    for z in range(N_BRANCHES):
        for part in range(2):
            kv_specs.append(pl.BlockSpec((seq, B_HEAD_DIM),
                                         lambda b, kh, z=z, part=part: (b, (2 * z + part) * B_KV_HEADS + kh)))
    return pl.pallas_call(
        _dilated_prompt_kernel,
        grid=(batch, B_KV_HEADS),
        in_specs=[pl.BlockSpec((None, 1, LANES), lambda b, kh: (kh, 0, 0))] + q_specs + kv_specs,
        out_specs=pl.BlockSpec((seq, gw), lambda b, kh: (b, kh)),
        out_shape=jax.ShapeDtypeStruct((batch * seq, B_HEADS * B_HEAD_DIM), BF16),
        scratch_shapes=[pltpu.VMEM((N_BRANCHES, group, seq, B_HEAD_DIM), F32),
                        pltpu.VMEM((N_BRANCHES, seq, LANES), F32)],
        compiler_params=_params("parallel", "parallel"),
        name="dilated_prompt",
    )(slopes, *([qp] * group), *([kv] * (2 * N_BRANCHES)))


def _dilated_sample_kernel(q_ref, new_ref, c0_ref, c1_ref, c2_ref, o_ref):
    caches = (c0_ref, c1_ref, c2_ref)
    nrow = c0_ref.shape[0]
    kw = B_KV_HEADS * B_HEAD_DIM
    scale = B_HEAD_DIM ** -0.5
    group = B_HEADS // B_KV_HEADS
    rowi = lax.broadcasted_iota(jnp.int32, (nrow, 1), 0)
    for h in range(B_HEADS):
        kh = h // group
        if h % group == 0:
            kcs = [caches[z][:, kh, :] for z in range(N_BRANCHES)]
            vcs = [caches[z][:, B_KV_HEADS + kh, :] for z in range(N_BRANCHES)]
        qh = q_ref[:, h * B_HEAD_DIM:(h + 1) * B_HEAD_DIM]
        outs, lses = [], []
        for z in range(N_BRANCHES):
            kc, vc = kcs[z], vcs[z]
            base = z * 2 * kw
            kn = new_ref[:, base + kh * B_HEAD_DIM:base + (kh + 1) * B_HEAD_DIM]
            vn = new_ref[:, base + kw + kh * B_HEAD_DIM:base + kw + (kh + 1) * B_HEAD_DIM]
            dist = ((nrow - rowi) * B_DILATIONS[z]).astype(F32)
            sc = jnp.sum(kc * qh, axis=-1, keepdims=True) * scale - _alibi_slope(h) * dist
            sn = jnp.sum(kn * qh, axis=-1, keepdims=True) * scale
            m = jnp.maximum(jnp.max(sc, axis=0, keepdims=True), sn)
            pc = jnp.exp(sc - m)
            pn = jnp.exp(sn - m)
            den = jnp.sum(pc, axis=0, keepdims=True) + pn
            outs.append((jnp.sum(pc * vc, axis=0, keepdims=True) + pn * vn) / den)
            lses.append(m + jnp.log(den))
        mm = jnp.maximum(jnp.maximum(lses[0], lses[1]), lses[2])
        es = [jnp.exp(l - mm) for l in lses]
        tot = es[0] + es[1] + es[2]
        mix = (es[0] / tot) * outs[0] + (es[1] / tot) * outs[1] + (es[2] / tot) * outs[2]
        o_ref[:, h * B_HEAD_DIM:(h + 1) * B_HEAD_DIM] = mix


def dilated_sample(qs, kv_new, caches):
    bsz, hd = qs.shape
    in_specs = [pl.BlockSpec((None, 1, hd), lambda b: (b, 0, 0)),
                pl.BlockSpec((None, 1, kv_new.shape[1]), lambda b: (b, 0, 0))]
    args = [qs.reshape(bsz, 1, hd), kv_new.reshape(bsz, 1, -1)]
    for z, cache in enumerate(caches):
        dil = B_DILATIONS[z]
        steps = B_WINDOWS[z] // dil
        assert cache.shape[1] == steps * dil
        args.append(cache.reshape(bsz, steps, dil, 2 * B_KV_HEADS, B_HEAD_DIM))
        in_specs.append(pl.BlockSpec((None, steps, None, 2 * B_KV_HEADS, B_HEAD_DIM), lambda b: (b, 0, 0, 0, 0)))
    out = pl.pallas_call(
        _dilated_sample_kernel,
        grid=(bsz,),
        in_specs=in_specs,
        out_specs=pl.BlockSpec((None, 1, hd), lambda b: (b, 0, 0)),
        out_shape=jax.ShapeDtypeStruct((bsz, 1, hd), F32),
        compiler_params=_params("parallel"),
        name="dilated_sample",
    )(*args)
    return out.reshape(bsz, hd).astype(BF16)


def _topk_rows(x, payload=None):
    r = x.shape[0]
    iota = lax.broadcasted_iota(jnp.int32, x.shape, 0).astype(F32)
    vals, idxs = [], []
    for _ in range(P_TOPK):
        m = jnp.max(x, axis=0, keepdims=True)
        am = jnp.min(jnp.where(x == m, iota, float(r)), axis=0, keepdims=True)
        hit = iota == am
        vals.append(m)
        if payload is None:
            idxs.append(am)
        else:
            idxs.append(jnp.max(jnp.where(hit, payload, 0.0), axis=0, keepdims=True))
        x = jnp.where(hit, NEG_INF, x)
    return jnp.concatenate(vals, axis=0), jnp.concatenate(idxs, axis=0)


def _staircase_pairs():
    return [(a, b) for a in range(P_TOPK) for b in range(P_TOPK) if (a + 1) * (b + 1) <= P_TOPK]


def _pick_rows(sel, x):
    hi, mid, lo = _split3(x)
    return _dot(sel, hi) + _dot(sel, mid) + _dot(sel, lo)


def _peer_route_kernel(q_ref, keys_ref, sel_ref, idx_ref, gate_ref, *, npair):
    sel_a, sel_b = sel_ref[0], sel_ref[1]
    nrow = sel_a.shape[0]
    pad = jnp.where(lax.broadcasted_iota(jnp.int32, (nrow, 1), 0) < npair, 0.0, NEG_INF)
    idx_rows, gate_rows = [], []
    for h in range(P_HEADS):
        sub = []
        for c in range(2):
            off = (h * 2 + c) * P_HALF
            qhc = q_ref[:, off:off + P_HALF].astype(BF16)
            s_t = _dot_nt(keys_ref[h, c].astype(BF16), qhc)
            sub.append(_topk_rows(s_t))
        (sv0, si0), (sv1, si1) = sub
        cand = _pick_rows(sel_a, sv0) + _pick_rows(sel_b, sv1) + pad
        cidx = _dot(sel_a, si0.astype(BF16)) * P_NKEYS + _dot(sel_b, si1.astype(BF16))
        top_s, eidx = _topk_rows(cand, cidx)
        e = jnp.exp(top_s - top_s[0:1])
        gate_rows.append(e / jnp.sum(e, axis=0, keepdims=True))
        idx_rows.append(eidx)
    idx_ref[...] = jnp.concatenate(idx_rows, axis=0).T.astype(jnp.int32)
    gate_ref[...] = jnp.concatenate(gate_rows, axis=0).T


def peer_route(q, keys):
    t, qd = q.shape
    tt = _row_tile(t, LANES)
    nsel = P_HEADS * P_TOPK
    pairs = _staircase_pairs()
    nrow = -(-len(pairs) // SUBLANES) * SUBLANES
    sel = np.zeros((2, nrow, P_TOPK), np.float32)
    for r, (a, b) in enumerate(pairs):
        sel[0, r, a] = 1.0
        sel[1, r, b] = 1.0
    return pl.pallas_call(
        functools.partial(_peer_route_kernel, npair=len(pairs)),
        grid=(t // tt,),
        in_specs=[pl.BlockSpec((tt, qd), lambda i: (i, 0)),
                  pl.BlockSpec(keys.shape, lambda i: (0, 0, 0, 0)),
                  pl.BlockSpec(sel.shape, lambda i: (0, 0, 0))],
        out_specs=[pl.BlockSpec((tt, nsel), lambda i: (i, 0)),
                   pl.BlockSpec((tt, nsel), lambda i: (i, 0))],
        out_shape=[jax.ShapeDtypeStruct((t, nsel), jnp.int32),
                   jax.ShapeDtypeStruct((t, nsel), F32)],
        compiler_params=_params("parallel"),
        name="peer_route",
    )(q, keys, jnp.asarray(sel, BF16))


def _gelu_exact(x):
    return 0.5 * x * (1.0 + lax.erf(x * (1.0 / math.sqrt(2.0))))


PEER_SLOTS = 8
PEER_AHEAD = PEER_SLOTS - 1
IDX_PAD = SUBLANES
PEER_TOKENS = 128
HI_HALF = -65536


def _bf16_bits_high(x):
    bits = lax.bitcast_convert_type(x, jnp.int32)
    return bits + 0x7FFF + jnp.bitwise_and(jnp.right_shift(bits, 16), 1)


def _pack_table_kernel(u_ref, v_ref, o_ref):
    nword = u_ref.shape[1] // (2 * LANES)
    for half, src in enumerate((u_ref, v_ref)):
        for j in range(nword):
            lo = _bf16_bits_high(src[:, 2 * j * LANES:(2 * j + 1) * LANES])
            hi = _bf16_bits_high(src[:, (2 * j + 1) * LANES:(2 * j + 2) * LANES])
            words = jnp.bitwise_or(jnp.bitwise_and(jnp.right_shift(lo, 16), 0xFFFF), jnp.bitwise_and(hi, HI_HALF))
            o_ref[:, half * nword + j, :] = words


def pack_expert_table(u, v, layer):
    _, e, d = u.shape
    te = _row_tile(e, 256)
    return pl.pallas_call(
        _pack_table_kernel,
        grid=(e // te,),
        in_specs=[pl.BlockSpec((None, te, d), lambda i: (layer, i, 0))] * 2,
        out_specs=pl.BlockSpec((te, d // LANES, LANES), lambda i: (i, 0, 0)),
        out_shape=jax.ShapeDtypeStruct((e, d // LANES, LANES), jnp.int32),
        compiler_params=_params("parallel"),
        name="pack_expert_table",
    )(u, v)


def _unpack_pair(words):
    lo = lax.bitcast_convert_type(jnp.left_shift(words, 16), F32)
    hi = lax.bitcast_convert_type(jnp.bitwise_and(words, HI_HALF), F32)
    return lo, hi


def _peer_expert_kernel(idx_hbm, gate_ref, h_ref, g_ref, fg_ref, tab_hbm, out_ref,
                        idx_smem, *scratch, final_norm):
    bufs = scratch[:PEER_SLOTS]
    xn_ref, idx_sems, row_sems = scratch[PEER_SLOTS:]
    tt, d = h_ref.shape
    nsel = gate_ref.shape[1]
    nword = d // (2 * LANES)
    per = nsel // (2 * nword)
    step = pl.program_id(0)
    nsteps = pl.num_programs(0)
    cur = lax.rem(step, 2)

    def idx_copy(s, slot):
        return pltpu.make_async_copy(idx_hbm.at[pl.ds(s * tt, tt + IDX_PAD)], idx_smem.at[slot], idx_sems.at[slot])

    @pl.when(step == 0)
    def _():
        idx_copy(0, 0).start()

    xn_ref[...] = _rms(h_ref[...], g_ref[...])
    idx_copy(step, cur).wait()

    @pl.when(step + 1 < nsteps)
    def _():
        idx_copy(step + 1, 1 - cur).start()

    def issue(trow, slot, k0, k1):
        for k in range(k0, k1):
            e = idx_smem[cur, trow, k]
            pltpu.make_async_copy(tab_hbm.at[e], bufs[slot].at[:, k], row_sems.at[slot]).start(priority=k % 2)

    def wait_rows(slot):
        pltpu.make_async_copy(bufs[slot], bufs[slot], row_sems.at[slot]).wait()

    @pl.when(step == 0)
    def _():
        for p in range(PEER_AHEAD):
            issue(p, p, 0, nsel)

    def token(t, s):
        nxt = (s + PEER_AHEAD) % PEER_SLOTS
        wait_rows(s)
        x = xn_ref[pl.ds(t, 1), :]
        acc = jnp.zeros((nsel, LANES), F32)
        for j in range(nword):
            lo, hi = _unpack_pair(bufs[s][j])
            acc = acc + lo * x[:, 2 * j * LANES:(2 * j + 1) * LANES]
            acc = acc + hi * x[:, (2 * j + 1) * LANES:(2 * j + 2) * LANES]
            issue(t + PEER_AHEAD, nxt, j * per, (j + 1) * per)
        act = jnp.sum(acc, axis=-1, keepdims=True)
        gate_col = jnp.broadcast_to(gate_ref[pl.ds(t, 1), :], (nsel, nsel)).T
        w = gate_col * _gelu_exact(act)
        pieces = []
        for j in range(nword):
            lo, hi = _unpack_pair(bufs[s][nword + j])
            pieces.append(jnp.sum(lo * w, axis=0, keepdims=True))
            pieces.append(jnp.sum(hi * w, axis=0, keepdims=True))
            issue(t + PEER_AHEAD, nxt, (nword + j) * per, (nword + j + 1) * per)
        out_ref[pl.ds(t, 1), :] = h_ref[pl.ds(t, 1), :] + jnp.concatenate(pieces, axis=1)

    def ring_turn(i, carry):
        for s in range(PEER_SLOTS):
            token(i * PEER_SLOTS + s, s)
        return carry

    lax.fori_loop(0, tt // PEER_SLOTS, ring_turn, 0)

    @pl.when(step == nsteps - 1)
    def _():
        for p in range(PEER_AHEAD):
            wait_rows(p)

    if final_norm:
        out_ref[...] = _rms(out_ref[...], fg_ref[...])


def peer_experts(h, gain, idx, gate, tab, final_gain=None):
    t, d = h.shape
    nsel = idx.shape[1]
    tt = _row_tile(t, PEER_TOKENS)
    assert tt % PEER_SLOTS == 0 and PEER_AHEAD <= IDX_PAD and nsel == LANES
    final_norm = final_gain is not None
    fg = (final_gain if final_norm else gain).reshape(1, d)
    idx = jnp.pad(idx, ((0, IDX_PAD), (0, 0)))
    return pl.pallas_call(
        functools.partial(_peer_expert_kernel, final_norm=final_norm),
        grid=(t // tt,),
        in_specs=[pl.BlockSpec(memory_space=pl.ANY),
                  pl.BlockSpec((tt, nsel), lambda i: (i, 0)),
                  pl.BlockSpec((tt, d), lambda i: (i, 0)),
                  pl.BlockSpec((1, d), lambda i: (0, 0)),
                  pl.BlockSpec((1, d), lambda i: (0, 0)),
                  pl.BlockSpec(memory_space=pl.ANY)],
        out_specs=pl.BlockSpec((tt, d), lambda i: (i, 0)),
        out_shape=jax.ShapeDtypeStruct((t, d), F32),
        scratch_shapes=[pltpu.SMEM((2, tt + IDX_PAD, nsel), jnp.int32)]
        + [pltpu.VMEM(tab.shape[1:2] + (nsel, LANES), jnp.int32) for _ in range(PEER_SLOTS)]
        + [pltpu.VMEM((tt, d), F32),
           pltpu.SemaphoreType.DMA((2,)),
           pltpu.SemaphoreType.DMA((PEER_SLOTS,))],
        compiler_params=_params("arbitrary"),
        name="peer_experts",
    )(idx, gate, h, gain.reshape(1, d), fg, tab)


def peer_layer(h, gain, wq, keys, uv, final_gain=None):
    q = norm_matmul(h, gain, wq)
    idx, gate = peer_route(q, keys)
    return peer_experts(h, gain, idx, gate, uv, final_gain)


def kernel(x_prompt, x_sample, state_hgrn, cache_win_r1, cache_win_r4, cache_win_r16, norm_mix_a, w_in_a, lb_logits, gnorm_a, w_out_a, norm_kv, w_kv, norm_mix_b, w_q_b, w_out_b, norm_ffn, peer_wq, peer_keys, peer_u, peer_v, norm_final):
    batch, seq, d = x_prompt.shape
    dec = x_sample.shape[0]
    n_a = w_in_a.shape[0]
    depth = norm_ffn.shape[0]
    caches = (cache_win_r1, cache_win_r4, cache_win_r16)
    lb_all = jnp.cumsum(jax.nn.softmax(lb_logits.astype(F32), axis=0), axis=0)
    kw2 = 2 * B_KV_HEADS * B_HEAD_DIM

    hp = x_prompt.reshape(batch * seq, d)
    hs = x_sample.reshape(dec, d)
    states_p, states_s = [], []
    kv_p = kv_s = None
    for layer in range(depth):
        if layer < n_a:
            w_in = w_in_a[layer].astype(BF16)
            w_out = w_out_a[layer].astype(BF16)
            proj_p = norm_matmul(hp, norm_mix_a[layer], w_in)
            o_p, st_p = hgrn_prompt(proj_p, lb_all[layer], gnorm_a[layer], batch, seq)
            hp = matmul_residual(o_p, w_out, hp)
            proj_s = norm_matmul(hs, norm_mix_a[layer], w_in)
            o_s, st_s = hgrn_step(proj_s, lb_all[layer], gnorm_a[layer], state_hgrn[:, layer])
            hs = matmul_residual(o_s, w_out, hs)
            states_p.append(st_p)
            states_s.append(st_s)
        else:
            li = layer - n_a
            if layer == n_a:
                wkv = w_kv.astype(BF16)
                kv_p = norm_matmul(hp, norm_kv, wkv)
                kv_s = norm_matmul(hs, norm_kv, wkv)
            wq = w_q_b[li].astype(BF16)
            w_out = w_out_b[li].astype(BF16)
            q_p = norm_matmul(hp, norm_mix_b[li], wq)
            mix_p = dilated_prompt(q_p, kv_p, batch, seq)
            hp = matmul_residual(mix_p, w_out, hp)
            q_s = norm_matmul(hs, norm_mix_b[li], wq)
            mix_s = dilated_sample(q_s, kv_s, caches)
            hs = matmul_residual(mix_s, w_out, hs)
        uv = pack_expert_table(peer_u, peer_v, layer)
        pwq = peer_wq[layer].astype(BF16)
        fin = norm_final if layer == depth - 1 else None
        hp = peer_layer(hp, norm_ffn[layer], pwq, peer_keys[layer], uv, fin)
        hs = peer_layer(hs, norm_ffn[layer], pwq, peer_keys[layer], uv, fin)

    def branch_rows(kv, nb, t, z, keep):
        rows = kv.reshape(nb, t, N_BRANCHES * kw2)[:, t - keep:, z * kw2:(z + 1) * kw2]
        return rows.reshape(nb, keep, 2, B_KV_HEADS, B_HEAD_DIM)

    rows_p = [branch_rows(kv_p, batch, seq, z, min(B_WINDOWS[z], seq)) for z in range(N_BRANCHES)]
    rows_s = [branch_rows(kv_s, dec, 1, z, 1) for z in range(N_BRANCHES)]
    return (hp.reshape(batch, seq, d), hs.reshape(dec, 1, d),
            jnp.stack(states_p, axis=1), rows_p[0], rows_p[1], rows_p[2],
            jnp.stack(states_s, axis=1), rows_s[0], rows_s[1], rows_s[2])
```

```python
import functools
import math

import jax
import jax.numpy as jnp
import numpy as np
from jax import lax
from jax.experimental import pallas as pl
from jax.experimental.pallas import tpu as pltpu

F32 = jnp.float32
BF16 = jnp.bfloat16
EPS = 1e-6

LANES = 128
SUBLANES = 8
VMEM_LIMIT = 56 * 1024 * 1024

A_HEADS = 16
A_DK = 128
A_CHUNK = 32
B_HEADS = 16
B_HEAD_DIM = 128
B_KV_HEADS = 4
B_WINDOWS = (128, 512, 2048)
B_DILATIONS = (1, 4, 16)
N_BRANCHES = 3
P_HEADS = 8
P_NKEYS = 128
P_TOPK = 16
P_HALF = 128
NEG_INF = float("-inf")


def _params(*sem):
    return pltpu.CompilerParams(dimension_semantics=sem, vmem_limit_bytes=VMEM_LIMIT)


def _dot(a, b):
    return jnp.dot(a, b, preferred_element_type=F32)


def _dot_nt(a, b):
    return lax.dot_general(a, b, (((1,), (1,)), ((), ())), preferred_element_type=F32)


def _dot_tn(a, b):
    return lax.dot_general(a, b, (((0,), (0,)), ((), ())), preferred_element_type=F32)


def _rms(x, g):
    ms = jnp.mean(x * x, axis=-1, keepdims=True)
    return x * lax.rsqrt(ms + EPS) * g


def _row_tile(m, want):
    t = min(m, want)
    while m % t:
        t //= 2
    return t


MM_ROWS = 1024
MM_COLS = 1024


def _norm_matmul_kernel(x_ref, g_ref, w_ref, o_ref, xn_ref):
    @pl.when(pl.program_id(1) == 0)
    def _():
        xn_ref[...] = _rms(x_ref[...], g_ref[...]).astype(BF16)

    o_ref[...] = _dot(xn_ref[...], w_ref[...])


def norm_matmul(x, g, w):
    m, k = x.shape
    n = w.shape[1]
    tm = _row_tile(m, MM_ROWS)
    tn = _row_tile(n, MM_COLS)
    return pl.pallas_call(
        _norm_matmul_kernel,
        grid=(m // tm, n // tn),
        in_specs=[pl.BlockSpec((tm, k), lambda i, j: (i, 0)),
                  pl.BlockSpec((1, k), lambda i, j: (0, 0)),
                  pl.BlockSpec((k, tn), lambda i, j: (0, j))],
        out_specs=pl.BlockSpec((tm, tn), lambda i, j: (i, j)),
        out_shape=jax.ShapeDtypeStruct((m, n), F32),
        scratch_shapes=[pltpu.VMEM((tm, k), BF16)],
        compiler_params=_params("parallel", "arbitrary"),
        name="norm_matmul",
    )(x, g.reshape(1, k), w)


def _matmul_res_kernel(a_ref, w_ref, r_ref, o_ref):
    o_ref[...] = r_ref[...] + _dot(a_ref[...], w_ref[...])


def matmul_residual(a, w, res):
    m, k = a.shape
    n = w.shape[1]
    tm = _row_tile(m, MM_ROWS)
    tn = _row_tile(n, MM_COLS)
    return pl.pallas_call(
        _matmul_res_kernel,
        grid=(m // tm, n // tn),
        in_specs=[pl.BlockSpec((tm, k), lambda i, j: (i, 0)),
                  pl.BlockSpec((k, tn), lambda i, j: (0, j)),
                  pl.BlockSpec((tm, tn), lambda i, j: (i, j))],
        out_specs=pl.BlockSpec((tm, tn), lambda i, j: (i, j)),
        out_shape=jax.ShapeDtypeStruct((m, n), F32),
        compiler_params=_params("parallel", "arbitrary"),
        name="matmul_residual",
    )(a, w, res)


def _split3(x):
    hi = x.astype(BF16)
    r1 = x - hi.astype(F32)
    mid = r1.astype(BF16)
    lo = (r1 - mid.astype(F32)).astype(BF16)
    return hi, mid, lo


def _gated_head_norm(o, gn, g):
    return _rms(o, gn) * jax.nn.sigmoid(g)


def _hgrn_prompt_kernel(q_ref, f_ref, i_ref, g_ref, lb_ref, gn_ref, o_ref, s_ref):
    seq = q_ref.shape[0]
    grp = LANES
    row = lax.broadcasted_iota(jnp.int32, (grp, grp), 0)
    col = lax.broadcasted_iota(jnp.int32, (grp, grp), 1)
    shift = A_CHUNK.bit_length() - 1
    same = jnp.right_shift(row, shift) == jnp.right_shift(col, shift)
    tri = same & (col <= row)
    l_tri = tri.astype(BF16)
    l_all = same.astype(BF16)
    gn = gn_ref[...]
    nhead = s_ref.shape[0]
    s_ref[...] = jnp.zeros(s_ref.shape, F32)

    def group(gi, carry):
        sl = pl.ds(pl.multiple_of(gi * grp, grp), grp)
        heads = range(nhead)
        cols = [slice(hh * A_DK, (hh + 1) * A_DK) for hh in heads]
        st = [s_ref[hh] for hh in heads]
        f = [lb_ref[:, cs] + (1.0 - lb_ref[:, cs]) * jax.nn.sigmoid(f_ref[sl, cs]) for cs in cols]
        k = [1.0 - fh for fh in f]
        parts = [_split3(jnp.log(fh)) for fh in f]
        b = [_dot(l_tri, p[0]) + _dot(l_tri, p[1]) + _dot(l_tri, p[2]) for p in parts]
        bt = [_dot(l_all, p[0]) + _dot(l_all, p[1]) + _dot(l_all, p[2]) for p in parts]
        qe = [(q_ref[sl, cs] * jnp.exp(bh)).astype(BF16) for cs, bh in zip(cols, b)]
        ke = [(kh * jnp.exp(-bh)).astype(BF16) for kh, bh in zip(k, b)]
        kd = [(kh * jnp.exp(bth - bh)).astype(BF16) for kh, bth, bh in zip(k, bt, b)]
        vb = [i_ref[sl, cs].astype(BF16) for cs in cols]
        att = [jnp.where(tri, _dot_nt(qh, kh), 0.0).astype(BF16) for qh, kh in zip(qe, ke)]
        o_intra = [_dot(ah, vh) for ah, vh in zip(att, vb)]
        dec_t = [jnp.exp(bth).T for bth in bt]
        outs = [[] for _ in heads]
        for c in range(grp // A_CHUNK):
            rows = slice(c * A_CHUNK, (c + 1) * A_CHUNK)
            for hh in heads:
                outs[hh].append(o_intra[hh][rows] + _dot(qe[hh][rows], st[hh].astype(BF16)))
            upd = [_dot_tn(kd[hh][rows], vb[hh][rows]) for hh in heads]
            st = [dec_t[hh][:, c * A_CHUNK:c * A_CHUNK + 1] * st[hh] + upd[hh] for hh in heads]
        for hh in heads:
            o = jnp.concatenate(outs[hh], axis=0)
            o_ref[sl, cols[hh]] = _gated_head_norm(o, gn, g_ref[sl, cols[hh]]).astype(BF16)
            s_ref[hh] = st[hh]
        return carry

    lax.fori_loop(0, seq // grp, group, 0)


HGRN_HEADS_PER_STEP = 4


def hgrn_prompt(proj, lb, gnorm, batch, seq):
    d = proj.shape[1] // 4
    dv = d // A_HEADS
    hb = HGRN_HEADS_PER_STEP
    ng = A_HEADS // hb

    def col(off):
        return pl.BlockSpec((seq, hb * dv), lambda b, h: (b, off * ng + h))

    return pl.pallas_call(
        _hgrn_prompt_kernel,
        grid=(batch, ng),
        in_specs=[col(0), col(1), col(2), col(3),
                  pl.BlockSpec((1, hb * A_DK), lambda b, h: (0, h)),
                  pl.BlockSpec((1, dv), lambda b, h: (0, 0))],
        out_specs=[pl.BlockSpec((seq, hb * dv), lambda b, h: (b, h)),
                   pl.BlockSpec((None, hb, A_DK, dv), lambda b, h: (b, h, 0, 0))],
        out_shape=[jax.ShapeDtypeStruct((batch * seq, d), BF16),
                   jax.ShapeDtypeStruct((batch, A_HEADS, A_DK, dv), F32)],
        compiler_params=_params("parallel", "parallel"),
        name="hgrn_prompt",
    )(proj, proj, proj, proj, lb.reshape(1, -1), gnorm.reshape(1, -1))


def _as_column(rowvec):
    n = rowvec.shape[1]
    return jnp.broadcast_to(rowvec, (n, n)).T


def _hgrn_step_kernel(p_ref, lb_ref, gn_ref, s_ref, o_ref, so_ref):
    nh = s_ref.shape[0]
    d = nh * A_DK
    for h in range(nh):
        cs = slice(h * A_DK, (h + 1) * A_DK)
        q = p_ref[:, h * A_DK:(h + 1) * A_DK]
        f_raw = p_ref[:, d + h * A_DK:d + (h + 1) * A_DK]
        v = p_ref[:, 2 * d + h * A_DK:2 * d + (h + 1) * A_DK]
        g = p_ref[:, 3 * d + h * A_DK:3 * d + (h + 1) * A_DK]
        f = lb_ref[:, cs] + (1.0 - lb_ref[:, cs]) * jax.nn.sigmoid(f_raw)
        s_new = _as_column(f) * s_ref[h] + _as_column(1.0 - f) * v
        so_ref[h] = s_new
        o = jnp.sum(_as_column(q) * s_new, axis=0, keepdims=True)
        o_ref[:, cs] = _gated_head_norm(o, gn_ref[...], g)


def hgrn_step(proj, lb, gnorm, state):
    bsz = proj.shape[0]
    d = proj.shape[1] // 4
    dv = d // A_HEADS
    assert dv == A_DK
    sspec = pl.BlockSpec((None, A_HEADS, A_DK, dv), lambda b: (b, 0, 0, 0))
    o, s_new = pl.pallas_call(
        _hgrn_step_kernel,
        grid=(bsz,),
        in_specs=[pl.BlockSpec((None, 1, 4 * d), lambda b: (b, 0, 0)),
                  pl.BlockSpec((1, d), lambda b: (0, 0)),
                  pl.BlockSpec((1, dv), lambda b: (0, 0)),
                  sspec],
        out_specs=[pl.BlockSpec((None, 1, d), lambda b: (b, 0, 0)), sspec],
        out_shape=[jax.ShapeDtypeStruct((bsz, 1, d), F32),
                   jax.ShapeDtypeStruct(state.shape, F32)],
        compiler_params=_params("parallel"),
        name="hgrn_step",
    )(proj.reshape(bsz, 1, 4 * d), lb.reshape(1, -1), gnorm.reshape(1, -1), state)
    return o.reshape(bsz, d).astype(BF16), s_new


def _alibi_slope(h):
    return 2.0 ** (-8.0 * (h + 1) / B_HEADS)


def _window_block(q_rows, keys, vals, slopes, bias_dist, valid):
    scale = B_HEAD_DIM ** -0.5
    s = [_dot_nt(qg.astype(BF16), keys) for qg in q_rows]
    s = [jnp.where(valid, sg * scale - slopes[:, g:g + 1] * bias_dist, NEG_INF) for g, sg in enumerate(s)]
    m = [jnp.max(sg, axis=-1, keepdims=True) for sg in s]
    p = [jnp.exp(sg - mg) for sg, mg in zip(s, m)]
    den = [jnp.sum(pg, axis=-1, keepdims=True) for pg in p]
    o = [_dot(pg.astype(BF16), vals) / dg for pg, dg in zip(p, den)]
    return [(og, mg + jnp.log(dg)) for og, mg, dg in zip(o, m, den)]


def _dilated_prompt_kernel(slope_ref, *refs):
    group = B_HEADS // B_KV_HEADS
    q_refs = refs[:group]
    kv_refs = refs[group:group + 2 * N_BRANCHES]
    out_ref, o_scr, lse_scr = refs[group + 2 * N_BRANCHES:]
    seq = out_ref.shape[0]
    slopes = slope_ref[...]

    for z in range(N_BRANCHES):
        dil = B_DILATIONS[z]
        steps = B_WINDOWS[z] // dil
        bq = min(steps, seq // dil)
        nblk = seq // (dil * bq)
        k_ref, v_ref = kv_refs[2 * z], kv_refs[2 * z + 1]
        lane = lax.broadcasted_iota(jnp.int32, (bq, LANES), 1)
        qa = lax.broadcasted_iota(jnp.int32, (bq, 2 * bq), 0)
        kb = lax.broadcasted_iota(jnp.int32, (bq, 2 * bq), 1)
        dist2 = bq + qa - kb
        near2 = (dist2 >= 0) & (dist2 <= steps)
        dist1 = dist2[:, bq:]
        near1 = near2[:, bq:]

        def store(rows, res):
            lse_tile = jnp.zeros((bq, LANES), F32)
            for g, (o, lse) in enumerate(res):
                o_scr[z, g, rows, :] = o
                lse_tile = jnp.where(lane == g, lse, lse_tile)
            lse_scr[z, rows, :] = lse_tile

        for c in range(dil):
            def rows_at(i):
                return pl.ds(c + i * (dil * bq), bq, stride=dil)

            first = rows_at(0)
            store(first, _window_block([q[first, :] for q in q_refs], k_ref[first, :].astype(BF16),
                                       v_ref[first, :].astype(BF16), slopes, (dist1 * dil).astype(F32), near1))

            def block(i, carry):
                cur, prev = rows_at(i), rows_at(i - 1)
                keys = jnp.concatenate([k_ref[prev, :], k_ref[cur, :]], axis=0).astype(BF16)
                vals = jnp.concatenate([v_ref[prev, :], v_ref[cur, :]], axis=0).astype(BF16)
                store(cur, _window_block([q[cur, :] for q in q_refs], keys, vals, slopes,
                                         (dist2 * dil).astype(F32), near2))
                return carry

            if nblk > 1:
                lax.fori_loop(1, nblk, block, 0)

    def mix(i, carry):
        rows = pl.ds(pl.multiple_of(i * LANES, LANES), LANES)
        l0, l1, l2 = lse_scr[0, rows, :], lse_scr[1, rows, :], lse_scr[2, rows, :]
        m = jnp.maximum(jnp.maximum(l0, l1), l2)
        e0, e1, e2 = jnp.exp(l0 - m), jnp.exp(l1 - m), jnp.exp(l2 - m)
        tot = e0 + e1 + e2
        w0, w1, w2 = e0 / tot, e1 / tot, e2 / tot
        for g in range(group):
            hs = slice(g * B_HEAD_DIM, (g + 1) * B_HEAD_DIM)
            out_ref[rows, hs] = (w0[:, g:g + 1] * o_scr[0, g, rows, :] + w1[:, g:g + 1] * o_scr[1, g, rows, :]
                                 + w2[:, g:g + 1] * o_scr[2, g, rows, :]).astype(BF16)
        return carry

    lax.fori_loop(0, seq // LANES, mix, 0)


def dilated_prompt(qp, kv, batch, seq):
    group = B_HEADS // B_KV_HEADS
    gw = group * B_HEAD_DIM
    assert seq % LANES == 0 and all(seq % min(B_WINDOWS[z], seq) == 0 for z in range(N_BRANCHES))
    slopes = jnp.exp2(-8.0 * jnp.arange(1, B_HEADS + 1, dtype=F32) / B_HEADS).reshape(B_KV_HEADS, group)
    slopes = jnp.pad(slopes, ((0, 0), (0, LANES - group))).reshape(B_KV_HEADS, 1, LANES)
    q_specs = [pl.BlockSpec((seq, B_HEAD_DIM), lambda b, kh, g=g: (b, kh * group + g)) for g in range(group)]
    kv_specs = []
    for z in range(N_BRANCHES):
        for part in range(2):
            kv_specs.append(pl.BlockSpec((seq, B_HEAD_DIM),
                                         lambda b, kh, z=z, part=part: (b, (2 * z + part) * B_KV_HEADS + kh)))
    return pl.pallas_call(
        _dilated_prompt_kernel,
        grid=(batch, B_KV_HEADS),
        in_specs=[pl.BlockSpec((None, 1, LANES), lambda b, kh: (kh, 0, 0))] + q_specs + kv_specs,
        out_specs=pl.BlockSpec((seq, gw), lambda b, kh: (b, kh)),
        out_shape=jax.ShapeDtypeStruct((batch * seq, B_HEADS * B_HEAD_DIM), BF16),
        scratch_shapes=[pltpu.VMEM((N_BRANCHES, group, seq, B_HEAD_DIM), F32),
                        pltpu.VMEM((N_BRANCHES, seq, LANES), F32)],
        compiler_params=_params("parallel", "parallel"),
        name="dilated_prompt",
    )(slopes, *([qp] * group), *([kv] * (2 * N_BRANCHES)))


def _dilated_sample_kernel(q_ref, new_ref, c0_ref, c1_ref, c2_ref, o_ref):
    caches = (c0_ref, c1_ref, c2_ref)
    nrow = c0_ref.shape[0]
    kw = B_KV_HEADS * B_HEAD_DIM
    scale = B_HEAD_DIM ** -0.5
    group = B_HEADS // B_KV_HEADS
    rowi = lax.broadcasted_iota(jnp.int32, (nrow, 1), 0)
    for h in range(B_HEADS):
        kh = h // group
        if h % group == 0:
            kcs = [caches[z][:, kh, :] for z in range(N_BRANCHES)]
            vcs = [caches[z][:, B_KV_HEADS + kh, :] for z in range(N_BRANCHES)]
        qh = q_ref[:, h * B_HEAD_DIM:(h + 1) * B_HEAD_DIM]
        outs, lses = [], []
        for z in range(N_BRANCHES):
            kc, vc = kcs[z], vcs[z]
            base = z * 2 * kw
            kn = new_ref[:, base + kh * B_HEAD_DIM:base + (kh + 1) * B_HEAD_DIM]
            vn = new_ref[:, base + kw + kh * B_HEAD_DIM:base + kw + (kh + 1) * B_HEAD_DIM]
            dist = ((nrow - rowi) * B_DILATIONS[z]).astype(F32)
            sc = jnp.sum(kc * qh, axis=-1, keepdims=True) * scale - _alibi_slope(h) * dist
            sn = jnp.sum(kn * qh, axis=-1, keepdims=True) * scale
            m = jnp.maximum(jnp.max(sc, axis=0, keepdims=True), sn)
            pc = jnp.exp(sc - m)
            pn = jnp.exp(sn - m)
            den = jnp.sum(pc, axis=0, keepdims=True) + pn
            outs.append((jnp.sum(pc * vc, axis=0, keepdims=True) + pn * vn) / den)
            lses.append(m + jnp.log(den))
        mm = jnp.maximum(jnp.maximum(lses[0], lses[1]), lses[2])
        es = [jnp.exp(l - mm) for l in lses]
        tot = es[0] + es[1] + es[2]
        mix = (es[0] / tot) * outs[0] + (es[1] / tot) * outs[1] + (es[2] / tot) * outs[2]
        o_ref[:, h * B_HEAD_DIM:(h + 1) * B_HEAD_DIM] = mix


def dilated_sample(qs, kv_new, caches):
    bsz, hd = qs.shape
    in_specs = [pl.BlockSpec((None, 1, hd), lambda b: (b, 0, 0)),
                pl.BlockSpec((None, 1, kv_new.shape[1]), lambda b: (b, 0, 0))]
    args = [qs.reshape(bsz, 1, hd), kv_new.reshape(bsz, 1, -1)]
    for z, cache in enumerate(caches):
        dil = B_DILATIONS[z]
        steps = B_WINDOWS[z] // dil
        assert cache.shape[1] == steps * dil
        args.append(cache.reshape(bsz, steps, dil, 2 * B_KV_HEADS, B_HEAD_DIM))
        in_specs.append(pl.BlockSpec((None, steps, None, 2 * B_KV_HEADS, B_HEAD_DIM), lambda b: (b, 0, 0, 0, 0)))
    out = pl.pallas_call(
        _dilated_sample_kernel,
        grid=(bsz,),
        in_specs=in_specs,
        out_specs=pl.BlockSpec((None, 1, hd), lambda b: (b, 0, 0)),
        out_shape=jax.ShapeDtypeStruct((bsz, 1, hd), F32),
        compiler_params=_params("parallel"),
        name="dilated_sample",
    )(*args)
    return out.reshape(bsz, hd).astype(BF16)


def _topk_rows(x, payload=None):
    r = x.shape[0]
    iota = lax.broadcasted_iota(jnp.int32, x.shape, 0).astype(F32)
    vals, idxs = [], []
    for _ in range(P_TOPK):
        m = jnp.max(x, axis=0, keepdims=True)
        am = jnp.min(jnp.where(x == m, iota, float(r)), axis=0, keepdims=True)
        hit = iota == am
        vals.append(m)
        if payload is None:
            idxs.append(am)
        else:
            idxs.append(jnp.max(jnp.where(hit, payload, 0.0), axis=0, keepdims=True))
        x = jnp.where(hit, NEG_INF, x)
    return jnp.concatenate(vals, axis=0), jnp.concatenate(idxs, axis=0)


def _staircase_pairs():
    return [(a, b) for a in range(P_TOPK) for b in range(P_TOPK) if (a + 1) * (b + 1) <= P_TOPK]


def _pick_rows(sel, x):
    hi, mid, lo = _split3(x)
    return _dot(sel, hi) + _dot(sel, mid) + _dot(sel, lo)


def _peer_route_kernel(q_ref, keys_ref, sel_ref, idx_ref, gate_ref, *, npair):
    sel_a, sel_b = sel_ref[0], sel_ref[1]
    nrow = sel_a.shape[0]
    pad = jnp.where(lax.broadcasted_iota(jnp.int32, (nrow, 1), 0) < npair, 0.0, NEG_INF)
    idx_rows, gate_rows = [], []
    for h in range(P_HEADS):
        sub = []
        for c in range(2):
            off = (h * 2 + c) * P_HALF
            qhc = q_ref[:, off:off + P_HALF].astype(BF16)
            s_t = _dot_nt(keys_ref[h, c].astype(BF16), qhc)
            sub.append(_topk_rows(s_t))
        (sv0, si0), (sv1, si1) = sub
        cand = _pick_rows(sel_a, sv0) + _pick_rows(sel_b, sv1) + pad
        cidx = _dot(sel_a, si0.astype(BF16)) * P_NKEYS + _dot(sel_b, si1.astype(BF16))
        top_s, eidx = _topk_rows(cand, cidx)
        e = jnp.exp(top_s - top_s[0:1])
        gate_rows.append(e / jnp.sum(e, axis=0, keepdims=True))
        idx_rows.append(eidx)
    idx_ref[...] = jnp.concatenate(idx_rows, axis=0).T.astype(jnp.int32)
    gate_ref[...] = jnp.concatenate(gate_rows, axis=0).T


def peer_route(q, keys):
    t, qd = q.shape
    tt = _row_tile(t, LANES)
    nsel = P_HEADS * P_TOPK
    pairs = _staircase_pairs()
    nrow = -(-len(pairs) // SUBLANES) * SUBLANES
    sel = np.zeros((2, nrow, P_TOPK), np.float32)
    for r, (a, b) in enumerate(pairs):
        sel[0, r, a] = 1.0
        sel[1, r, b] = 1.0
    return pl.pallas_call(
        functools.partial(_peer_route_kernel, npair=len(pairs)),
        grid=(t // tt,),
        in_specs=[pl.BlockSpec((tt, qd), lambda i: (i, 0)),
                  pl.BlockSpec(keys.shape, lambda i: (0, 0, 0, 0)),
                  pl.BlockSpec(sel.shape, lambda i: (0, 0, 0))],
        out_specs=[pl.BlockSpec((tt, nsel), lambda i: (i, 0)),
                   pl.BlockSpec((tt, nsel), lambda i: (i, 0))],
        out_shape=[jax.ShapeDtypeStruct((t, nsel), jnp.int32),
                   jax.ShapeDtypeStruct((t, nsel), F32)],
        compiler_params=_params("parallel"),
        name="peer_route",
    )(q, keys, jnp.asarray(sel, BF16))


def _gelu_exact(x):
    return 0.5 * x * (1.0 + lax.erf(x * (1.0 / math.sqrt(2.0))))


PEER_SLOTS = 8
PEER_AHEAD = PEER_SLOTS - 1
IDX_PAD = SUBLANES
PEER_TOKENS = 128


def _slab_table_kernel(u_ref, v_ref, o_ref):
    nchunk = u_ref.shape[1] // LANES
    for half, src in enumerate((u_ref, v_ref)):
        for j in range(nchunk):
            o_ref[:, half * nchunk + j, :] = src[:, j * LANES:(j + 1) * LANES]


def pack_expert_table(u, v, layer):
    _, e, d = u.shape
    te = _row_tile(e, 256)
    return pl.pallas_call(
        _slab_table_kernel,
        grid=(e // te,),
        in_specs=[pl.BlockSpec((None, te, d), lambda i: (layer, i, 0))] * 2,
        out_specs=pl.BlockSpec((te, 2 * d // LANES, LANES), lambda i: (i, 0, 0)),
        out_shape=jax.ShapeDtypeStruct((e, 2 * d // LANES, LANES), F32),
        compiler_params=_params("parallel"),
        name="pack_expert_table",
    )(u, v)


def _peer_expert_kernel(idx_hbm, gate_ref, h_ref, g_ref, fg_ref, tab_hbm, out_ref,
                        idx_smem, *scratch, final_norm):
    bufs = scratch[:PEER_SLOTS]
    xn_ref, idx_sems, row_sems = scratch[PEER_SLOTS:]
    tt, d = h_ref.shape
    nsel = gate_ref.shape[1]
    nchunk = d // LANES
    per = nsel // (2 * nchunk)
    step = pl.program_id(0)
    nsteps = pl.num_programs(0)
    cur = lax.rem(step, 2)

    def idx_copy(s, slot):
        return pltpu.make_async_copy(idx_hbm.at[pl.ds(s * tt, tt + IDX_PAD)], idx_smem.at[slot], idx_sems.at[slot])

    @pl.when(step == 0)
    def _():
        idx_copy(0, 0).start()

    xn_ref[...] = _rms(h_ref[...], g_ref[...])
    idx_copy(step, cur).wait()

    @pl.when(step + 1 < nsteps)
    def _():
        idx_copy(step + 1, 1 - cur).start()

    def issue(trow, slot, k0, k1):
        for k in range(k0, k1):
            e = idx_smem[cur, trow, k]
            pltpu.make_async_copy(tab_hbm.at[e], bufs[slot].at[:, k], row_sems.at[slot]).start(priority=k % 2)

    def wait_rows(slot):
        pltpu.make_async_copy(bufs[slot], bufs[slot], row_sems.at[slot]).wait()

    @pl.when(step == 0)
    def _():
        for p in range(PEER_AHEAD):
            issue(p, p, 0, nsel)

    def token(t, s):
        nxt = (s + PEER_AHEAD) % PEER_SLOTS
        wait_rows(s)
        x = xn_ref[pl.ds(t, 1), :]
        acc = jnp.zeros((nsel, LANES), F32)
        for j in range(nchunk):
            acc = acc + bufs[s][j] * x[:, j * LANES:(j + 1) * LANES]
            issue(t + PEER_AHEAD, nxt, j * per, (j + 1) * per)
        act = jnp.sum(acc, axis=-1, keepdims=True)
        gate_col = jnp.broadcast_to(gate_ref[pl.ds(t, 1), :], (nsel, nsel)).T
        w = gate_col * _gelu_exact(act)
        pieces = []
        for j in range(nchunk):
            pieces.append(jnp.sum(bufs[s][nchunk + j] * w, axis=0, keepdims=True))
            issue(t + PEER_AHEAD, nxt, (nchunk + j) * per, (nchunk + j + 1) * per)
        out_ref[pl.ds(t, 1), :] = h_ref[pl.ds(t, 1), :] + jnp.concatenate(pieces, axis=1)

    def ring_turn(i, carry):
        for s in range(PEER_SLOTS):
            token(i * PEER_SLOTS + s, s)
        return carry

    lax.fori_loop(0, tt // PEER_SLOTS, ring_turn, 0)

    @pl.when(step == nsteps - 1)
    def _():
        for p in range(PEER_AHEAD):
            wait_rows(p)

    if final_norm:
        out_ref[...] = _rms(out_ref[...], fg_ref[...])


def peer_experts(h, gain, idx, gate, tab, final_gain=None):
    t, d = h.shape
    nsel = idx.shape[1]
    tt = _row_tile(t, PEER_TOKENS)
    assert tt % PEER_SLOTS == 0 and PEER_AHEAD <= IDX_PAD and nsel == LANES
    final_norm = final_gain is not None
    fg = (final_gain if final_norm else gain).reshape(1, d)
    idx = jnp.pad(idx, ((0, IDX_PAD), (0, 0)))
    return pl.pallas_call(
        functools.partial(_peer_expert_kernel, final_norm=final_norm),
        grid=(t // tt,),
        in_specs=[pl.BlockSpec(memory_space=pl.ANY),
                  pl.BlockSpec((tt, nsel), lambda i: (i, 0)),
                  pl.BlockSpec((tt, d), lambda i: (i, 0)),
                  pl.BlockSpec((1, d), lambda i: (0, 0)),
                  pl.BlockSpec((1, d), lambda i: (0, 0)),
                  pl.BlockSpec(memory_space=pl.ANY)],
        out_specs=pl.BlockSpec((tt, d), lambda i: (i, 0)),
        out_shape=jax.ShapeDtypeStruct((t, d), F32),
        scratch_shapes=[pltpu.SMEM((2, tt + IDX_PAD, nsel), jnp.int32)]
        + [pltpu.VMEM(tab.shape[1:2] + (nsel, LANES), F32) for _ in range(PEER_SLOTS)]
        + [pltpu.VMEM((tt, d), F32),
           pltpu.SemaphoreType.DMA((2,)),
           pltpu.SemaphoreType.DMA((PEER_SLOTS,))],
        compiler_params=_params("arbitrary"),
        name="peer_experts",
    )(idx, gate, h, gain.reshape(1, d), fg, tab)


def peer_layer(h, gain, wq, keys, uv, final_gain=None):
    q = norm_matmul(h, gain, wq)
    idx, gate = peer_route(q, keys)
    return peer_experts(h, gain, idx, gate, uv, final_gain)


def kernel(x_prompt, x_sample, state_hgrn, cache_win_r1, cache_win_r4, cache_win_r16, norm_mix_a, w_in_a, lb_logits, gnorm_a, w_out_a, norm_kv, w_kv, norm_mix_b, w_q_b, w_out_b, norm_ffn, peer_wq, peer_keys, peer_u, peer_v, norm_final):
    batch, seq, d = x_prompt.shape
    dec = x_sample.shape[0]
    n_a = w_in_a.shape[0]
    depth = norm_ffn.shape[0]
    caches = (cache_win_r1, cache_win_r4, cache_win_r16)
    lb_all = jnp.cumsum(jax.nn.softmax(lb_logits.astype(F32), axis=0), axis=0)
    kw2 = 2 * B_KV_HEADS * B_HEAD_DIM

    hp = x_prompt.reshape(batch * seq, d)
    hs = x_sample.reshape(dec, d)
    states_p, states_s = [], []
    kv_p = kv_s = None
    for layer in range(depth):
        if layer < n_a:
            w_in = w_in_a[layer].astype(BF16)
            w_out = w_out_a[layer].astype(BF16)
            proj_p = norm_matmul(hp, norm_mix_a[layer], w_in)
            o_p, st_p = hgrn_prompt(proj_p, lb_all[layer], gnorm_a[layer], batch, seq)
            hp = matmul_residual(o_p, w_out, hp)
            proj_s = norm_matmul(hs, norm_mix_a[layer], w_in)
            o_s, st_s = hgrn_step(proj_s, lb_all[layer], gnorm_a[layer], state_hgrn[:, layer])
            hs = matmul_residual(o_s, w_out, hs)
            states_p.append(st_p)
            states_s.append(st_s)
        else:
            li = layer - n_a
            if layer == n_a:
                wkv = w_kv.astype(BF16)
                kv_p = norm_matmul(hp, norm_kv, wkv)
                kv_s = norm_matmul(hs, norm_kv, wkv)
            wq = w_q_b[li].astype(BF16)
            w_out = w_out_b[li].astype(BF16)
            q_p = norm_matmul(hp, norm_mix_b[li], wq)
            mix_p = dilated_prompt(q_p, kv_p, batch, seq)
            hp = matmul_residual(mix_p, w_out, hp)
            q_s = norm_matmul(hs, norm_mix_b[li], wq)
            mix_s = dilated_sample(q_s, kv_s, caches)
            hs = matmul_residual(mix_s, w_out, hs)
        uv = pack_expert_table(peer_u, peer_v, layer)
        pwq = peer_wq[layer].astype(BF16)
        fin = norm_final if layer == depth - 1 else None
        hp = peer_layer(hp, norm_ffn[layer], pwq, peer_keys[layer], uv, fin)
        hs = peer_layer(hs, norm_ffn[layer], pwq, peer_keys[layer], uv, fin)

    def branch_rows(kv, nb, t, z, keep):
        rows = kv.reshape(nb, t, N_BRANCHES * kw2)[:, t - keep:, z * kw2:(z + 1) * kw2]
        return rows.reshape(nb, keep, 2, B_KV_HEADS, B_HEAD_DIM)

    rows_p = [branch_rows(kv_p, batch, seq, z, min(B_WINDOWS[z], seq)) for z in range(N_BRANCHES)]
    rows_s = [branch_rows(kv_s, dec, 1, z, 1) for z in range(N_BRANCHES)]
    return (hp.reshape(batch, seq, d), hs.reshape(dec, 1, d),
            jnp.stack(states_p, axis=1), rows_p[0], rows_p[1], rows_p[2],
            jnp.stack(states_s, axis=1), rows_s[0], rows_s[1], rows_s[2])
```
